```python
import math
import jax, jax.numpy as jnp
from jax import lax
import numpy as np

D_MODEL = 2048
BATCH = 2
SEQ = 16384
DEPTH = 2

N_MIXERS = 2
N_GDN_LAYERS = (DEPTH + N_MIXERS - 1) // N_MIXERS
N_SC_LAYERS = DEPTH // N_MIXERS
EPS = 1e-6

GDN_QK_HEADS = 16
GDN_V_HEADS = 32
GDN_HEAD_DIM = 128
GDN_KEY_DIM = GDN_QK_HEADS * GDN_HEAD_DIM
GDN_VAL_DIM = GDN_V_HEADS * GDN_HEAD_DIM
GDN_QKV_DIM = 2 * GDN_KEY_DIM + GDN_VAL_DIM
GDN_IN_DIM = GDN_QKV_DIM + GDN_VAL_DIM + 2 * GDN_V_HEADS
GDN_CONV = 4
GDN_CHUNK = 64

SC_WIDTH = 3

N_GROUPS = 8
EXPERTS_PER_GROUP = 8
N_EXPERTS = N_GROUPS * EXPERTS_PER_GROUP
TOP_K = 2
D_EXPERT = 512
MOE_BLOCK = 256

kernel_name = "hybrid_gdn_shortconv_hmoe"


def rms_norm(x, w):
    xf = x.astype(jnp.float32)
    y = xf * lax.rsqrt(jnp.mean(xf * xf, axis=-1, keepdims=True) + EPS)
    return (y * w.astype(jnp.float32)).astype(x.dtype)


def causal_depthwise_conv(x, w):
    width = w.shape[0]
    return lax.conv_general_dilated(
        x, w[:, None, :].astype(x.dtype), window_strides=(1,),
        padding=[(width - 1, 0)], dimension_numbers=("NWC", "WIO", "NWC"),
        feature_group_count=x.shape[-1])


def l2norm(x):
    return x * lax.rsqrt(jnp.sum(x * x, axis=-1, keepdims=True) + EPS)


def chunk_gated_delta_rule(q, k, v, g, beta):
    b, h, s, dk = q.shape
    dv = v.shape[-1]
    c = GDN_CHUNK
    n = s // c
    q = q * (dk ** -0.5)
    rs = lambda t: t.reshape(b, h, n, c, *t.shape[3:])
    q, k, v, g, beta = rs(q), rs(k), rs(v), rs(g), rs(beta)
    g = jnp.cumsum(g, axis=-1)
    tril = jnp.tril(jnp.ones((c, c), dtype=bool))
    strict = jnp.tril(jnp.ones((c, c), dtype=bool), -1)
    decay = jnp.exp(jnp.where(tril, g[..., :, None] - g[..., None, :], -jnp.inf))
    k_beta = k * beta[..., None]
    lower = jnp.where(strict, jnp.einsum("bhncd,bhnmd->bhncm", k_beta, k) * decay, 0.0)
    rhs = jnp.concatenate([v * beta[..., None], k_beta * jnp.exp(g)[..., None]], axis=-1)
    sol = lax.linalg.triangular_solve(lower, rhs, left_side=True, lower=True,
                                      unit_diagonal=True)
    u, w = sol[..., :dv], sol[..., dv:]
    attn_qk = jnp.where(tril, jnp.einsum("bhncd,bhnmd->bhncm", q, k) * decay, 0.0)
    q_dec = q * jnp.exp(g)[..., None]
    g_last = g[..., -1]
    k_dec = k * jnp.exp(g_last[..., None] - g)[..., None]
    to_chunk_major = lambda t: jnp.moveaxis(t, 2, 0)
    xs = (to_chunk_major(q_dec), to_chunk_major(k_dec), to_chunk_major(u),
          to_chunk_major(w), to_chunk_major(attn_qk), to_chunk_major(g_last))

    def step(state, inp):
        qd, kd, uc, wc, aqk, gl = inp
        v_new = uc - jnp.einsum("bhck,bhkv->bhcv", wc, state)
        o = (jnp.einsum("bhck,bhkv->bhcv", qd, state)
             + jnp.einsum("bhcm,bhmv->bhcv", aqk, v_new))
        state = (state * jnp.exp(gl)[..., None, None]
                 + jnp.einsum("bhck,bhcv->bhkv", kd, v_new))
        return state, o

    state0 = jnp.zeros((b, h, dk, dv), jnp.float32)
    _, o = lax.scan(step, state0, xs)
    return jnp.transpose(o, (1, 0, 3, 2, 4)).reshape(b, s, h, dv)


def gated_deltanet_mixer(x, w_in, conv_w, a_log, dt_bias, norm_w, w_out):
    b, s, _ = x.shape
    proj = x @ w_in
    qkv = jax.nn.silu(causal_depthwise_conv(proj[..., :GDN_QKV_DIM], conv_w))
    z = proj[..., GDN_QKV_DIM:GDN_QKV_DIM + GDN_VAL_DIM]
    beta_raw = proj[..., GDN_QKV_DIM + GDN_VAL_DIM:GDN_QKV_DIM + GDN_VAL_DIM + GDN_V_HEADS]
    a_raw = proj[..., GDN_QKV_DIM + GDN_VAL_DIM + GDN_V_HEADS:]
    rep = GDN_V_HEADS // GDN_QK_HEADS
    q = qkv[..., :GDN_KEY_DIM].reshape(b, s, GDN_QK_HEADS, GDN_HEAD_DIM)
    k = qkv[..., GDN_KEY_DIM:2 * GDN_KEY_DIM].reshape(b, s, GDN_QK_HEADS, GDN_HEAD_DIM)
    v = qkv[..., 2 * GDN_KEY_DIM:].reshape(b, s, GDN_V_HEADS, GDN_HEAD_DIM)
    q = l2norm(jnp.repeat(q, rep, axis=2).astype(jnp.float32))
    k = l2norm(jnp.repeat(k, rep, axis=2).astype(jnp.float32))
    v = v.astype(jnp.float32)
    beta = jax.nn.sigmoid(beta_raw.astype(jnp.float32))
    g = -jnp.exp(a_log.astype(jnp.float32)) * jax.nn.softplus(
        a_raw.astype(jnp.float32) + dt_bias.astype(jnp.float32))
    bhs = lambda t: jnp.moveaxis(t, 2, 1)
    o = chunk_gated_delta_rule(bhs(q), bhs(k), bhs(v), bhs(g), bhs(beta))
    o = o * lax.rsqrt(jnp.mean(o * o, axis=-1, keepdims=True) + EPS) * norm_w.astype(jnp.float32)
    o = o * jax.nn.silu(z.reshape(b, s, GDN_V_HEADS, GDN_HEAD_DIM).astype(jnp.float32))
    return o.reshape(b, s, GDN_VAL_DIM).astype(x.dtype) @ w_out


def short_conv_mixer(x, w_in, conv_w, w_out):
    gate_b, gate_c, h = jnp.split(x @ w_in, 3, axis=-1)
    return (gate_b * causal_depthwise_conv(gate_c * h, conv_w)) @ w_out


def hierarchical_moe(x, w_group, b_group, w_expert, b_expert, w_gu, w_down):
    b, s, d = x.shape
    t = b * s
    xt = x.reshape(t, d)
    xf = xt.astype(jnp.float32)
    group_logits = xf @ w_group.astype(jnp.float32) + b_group.astype(jnp.float32)
    group = jnp.argmax(group_logits, axis=-1).astype(jnp.int32)
    p_group = jnp.take_along_axis(jax.nn.softmax(group_logits, axis=-1), group[:, None], axis=1)
    expert_logits = (xf @ w_expert.astype(jnp.float32) + b_expert.astype(jnp.float32)
                     ).reshape(t, N_GROUPS, EXPERTS_PER_GROUP)
    within = jnp.take_along_axis(expert_logits, group[:, None, None], axis=1)[:, 0]
    top_p, top_i = lax.top_k(jax.nn.softmax(within, axis=-1), TOP_K)
    gate = p_group * top_p / jnp.sum(top_p, axis=-1, keepdims=True)
    expert_id = group[:, None] * EXPERTS_PER_GROUP + top_i.astype(jnp.int32)

    a = t * TOP_K
    e_flat = expert_id.reshape(a)
    g_flat = gate.reshape(a)
    tok_flat = jnp.arange(a, dtype=jnp.int32) // TOP_K
    order = jnp.argsort(e_flat)
    e_sorted = e_flat[order]
    counts = jnp.bincount(e_flat, length=N_EXPERTS)
    padded = (counts + MOE_BLOCK - 1) // MOE_BLOCK * MOE_BLOCK
    start = jnp.cumsum(counts) - counts
    pend = jnp.cumsum(padded)
    pstart = pend - padded
    dest = pstart[e_sorted] + jnp.arange(a, dtype=jnp.int32) - start[e_sorted]
    n_blocks = (a + MOE_BLOCK - 1) // MOE_BLOCK + N_EXPERTS
    cap = n_blocks * MOE_BLOCK
    slot_tok = jnp.zeros((cap,), jnp.int32).at[dest].set(tok_flat[order])
    slot_gate = jnp.zeros((cap,), jnp.float32).at[dest].set(g_flat[order])
    block_start = jnp.arange(n_blocks, dtype=jnp.int32) * MOE_BLOCK
    block_expert = jnp.minimum(jnp.searchsorted(pend, block_start, side="right"),
                               N_EXPERTS - 1).astype(jnp.int32)

    def expert_block(args):
        tok, e = args
        xb = xt[tok]
        gt, up = jnp.split(xb @ w_gu[e], 2, axis=-1)
        return (jax.nn.silu(gt) * up) @ w_down[e]

    out = lax.map(expert_block, (slot_tok.reshape(n_blocks, MOE_BLOCK), block_expert))
    out = out.reshape(cap, d) * slot_gate[:, None].astype(out.dtype)
    y = jax.ops.segment_sum(out, slot_tok, num_segments=t)
    return y.reshape(b, s, d).astype(x.dtype)


def setup_inputs(seed: int = 0) -> dict:
    key = jax.random.key(seed)
    ks = jax.random.split(key, 20)
    f32 = jnp.float32
    nrm = lambda k, shape, scale: jax.random.normal(k, shape, f32) * scale
    dt = jnp.exp(jax.random.uniform(ks[5], (N_GDN_LAYERS, GDN_V_HEADS), f32,
                                    minval=math.log(1e-3), maxval=math.log(0.1)))
    return {
        "x": nrm(ks[0], (BATCH, SEQ, D_MODEL), 1.0),
        "norm_mix": 1.0 + nrm(ks[1], (DEPTH, D_MODEL), 0.02),
        "norm_ffn": 1.0 + nrm(ks[2], (DEPTH, D_MODEL), 0.02),
        "gdn_w_in": nrm(ks[3], (N_GDN_LAYERS, D_MODEL, GDN_IN_DIM), D_MODEL ** -0.5),
        "gdn_conv_w": nrm(ks[4], (N_GDN_LAYERS, GDN_CONV, GDN_QKV_DIM), GDN_CONV ** -0.5),
        "gdn_a_log": jnp.log(jax.random.uniform(ks[6], (N_GDN_LAYERS, GDN_V_HEADS), f32,
                                                minval=1.0, maxval=16.0)),
        "gdn_dt_bias": dt + jnp.log(-jnp.expm1(-dt)),
        "gdn_norm_w": 1.0 + nrm(ks[7], (N_GDN_LAYERS, GDN_HEAD_DIM), 0.02),
        "gdn_w_out": nrm(ks[8], (N_GDN_LAYERS, GDN_VAL_DIM, D_MODEL), GDN_VAL_DIM ** -0.5),
        "sc_w_in": nrm(ks[9], (N_SC_LAYERS, D_MODEL, 3 * D_MODEL), D_MODEL ** -0.5),
        "sc_conv_w": nrm(ks[10], (N_SC_LAYERS, SC_WIDTH, D_MODEL), SC_WIDTH ** -0.5),
        "sc_w_out": nrm(ks[11], (N_SC_LAYERS, D_MODEL, D_MODEL), D_MODEL ** -0.5),
        "moe_w_group": nrm(ks[12], (DEPTH, D_MODEL, N_GROUPS), D_MODEL ** -0.5),
        "moe_b_group": nrm(ks[13], (DEPTH, N_GROUPS), 0.01),
        "moe_w_expert": nrm(ks[14], (DEPTH, D_MODEL, N_EXPERTS), D_MODEL ** -0.5),
        "moe_b_expert": nrm(ks[15], (DEPTH, N_EXPERTS), 0.01),
        "moe_w_gu": nrm(ks[16], (DEPTH, N_EXPERTS, D_MODEL, 2 * D_EXPERT), D_MODEL ** -0.5),
        "moe_w_down": nrm(ks[17], (DEPTH, N_EXPERTS, D_EXPERT, D_MODEL), D_EXPERT ** -0.5),
        "norm_final": 1.0 + nrm(ks[18], (D_MODEL,), 0.02),
    }


def reference(x, norm_mix, norm_ffn, gdn_w_in, gdn_conv_w, gdn_a_log, gdn_dt_bias,
              gdn_norm_w, gdn_w_out, sc_w_in, sc_conv_w, sc_w_out, moe_w_group,
              moe_b_group, moe_w_expert, moe_b_expert, moe_w_gu, moe_w_down, norm_final):
    h = x
    for i in range(DEPTH):
        j = i // N_MIXERS
        hn = rms_norm(h, norm_mix[i])
        if i % N_MIXERS == 0:
            h = h + gated_deltanet_mixer(hn, gdn_w_in[j], gdn_conv_w[j], gdn_a_log[j],
                                         gdn_dt_bias[j], gdn_norm_w[j], gdn_w_out[j])
        else:
            h = h + short_conv_mixer(hn, sc_w_in[j], sc_conv_w[j], sc_w_out[j])
        h = h + hierarchical_moe(rms_norm(h, norm_ffn[i]), moe_w_group[i], moe_b_group[i],
                                 moe_w_expert[i], moe_b_expert[i], moe_w_gu[i], moe_w_down[i])
    return rms_norm(h, norm_final)
```

```python
import functools

import jax
import jax.numpy as jnp
from jax import lax
from jax.experimental import pallas as pl
from jax.experimental.pallas import tpu as pltpu

EPS = 1e-6
F32 = jnp.float32
BF16 = jnp.bfloat16
HIGHEST = lax.Precision.HIGHEST

QK_HEADS = 16
V_HEADS = 32
HEAD_DIM = 128
KEY_DIM = QK_HEADS * HEAD_DIM
VAL_DIM = V_HEADS * HEAD_DIM
QKV_DIM = 2 * KEY_DIM + VAL_DIM
MAIN_DIM = QKV_DIM + VAL_DIM
GDN_CONV = 4
CHUNK = 64
HALO = 8

SC_WIDTH = 3

N_GROUPS = 8
EXPERTS_PER_GROUP = 8
N_EXPERTS = N_GROUPS * EXPERTS_PER_GROUP
D_EXPERT = 512
MOE_BLOCK = 256

VMEM_LIMIT = 56 * 1024 * 1024


def _cparams(sem):
    return pltpu.CompilerParams(dimension_semantics=sem, vmem_limit_bytes=VMEM_LIMIT)


def _rms(x, gain):
    return x * lax.rsqrt(jnp.mean(x * x, axis=-1, keepdims=True) + EPS) * gain


def _sigmoid(x):
    return 1.0 / (1.0 + jnp.exp(-x))


def _softplus(x):
    return jnp.maximum(x, 0.0) + jnp.log1p(jnp.exp(-jnp.abs(x)))


def _norm_matmul_kernel(x_ref, g_ref, w_ref, o_ref, xn_ref):
    @pl.when(pl.program_id(1) == 0)
    def _():
        xn_ref[...] = _rms(x_ref[...], g_ref[...]).astype(BF16)

    o_ref[...] = jnp.dot(xn_ref[...], w_ref[...], preferred_element_type=F32).astype(o_ref.dtype)


def _norm_matmul(x, gain, w, *, tm, tn, out_dtype):
    t, k = x.shape
    n = w.shape[1]
    return pl.pallas_call(
        _norm_matmul_kernel,
        out_shape=jax.ShapeDtypeStruct((t, n), out_dtype),
        grid=(t // tm, n // tn),
        in_specs=[
            pl.BlockSpec((tm, k), lambda i, j: (i, 0)),
            pl.BlockSpec((1, k), lambda i, j: (0, 0)),
            pl.BlockSpec((k, tn), lambda i, j: (0, j)),
        ],
        out_specs=pl.BlockSpec((tm, tn), lambda i, j: (i, j)),
        scratch_shapes=[pltpu.VMEM((tm, k), BF16)],
        compiler_params=_cparams(("parallel", "arbitrary")),
        name="norm_matmul",
    )(x, gain.reshape(1, k), w)


def _norm_side_kernel(x_ref, g_ref, w_ref, o_ref):
    xn = _rms(x_ref[...], g_ref[...])
    o_ref[...] = jnp.dot(xn, w_ref[...], precision=HIGHEST, preferred_element_type=F32)


def _norm_side(x, gain, w, *, tm):
    t, k = x.shape
    n = w.shape[1]
    return pl.pallas_call(
        _norm_side_kernel,
        out_shape=jax.ShapeDtypeStruct((t, n), F32),
        grid=(t // tm,),
        in_specs=[
            pl.BlockSpec((tm, k), lambda i: (i, 0)),
            pl.BlockSpec((1, k), lambda i: (0, 0)),
            pl.BlockSpec((k, n), lambda i: (0, 0)),
        ],
        out_specs=pl.BlockSpec((tm, n), lambda i: (i, 0)),
        compiler_params=_cparams(("parallel",)),
        name="norm_side",
    )(x, gain.reshape(1, k), w)


def _matmul_res_kernel(a_ref, w_ref, r_ref, o_ref):
    o_ref[...] = r_ref[...] + jnp.dot(a_ref[...], w_ref[...], preferred_element_type=F32)


def _matmul_res(a, w, res, *, tm, tn):
    t, k = a.shape
    n = w.shape[1]
    return pl.pallas_call(
        _matmul_res_kernel,
        out_shape=jax.ShapeDtypeStruct((t, n), F32),
        grid=(t // tm, n // tn),
        in_specs=[
            pl.BlockSpec((tm, k), lambda i, j: (i, 0)),
            pl.BlockSpec((k, tn), lambda i, j: (0, j)),
            pl.BlockSpec((tm, tn), lambda i, j: (i, j)),
        ],
        out_specs=pl.BlockSpec((tm, tn), lambda i, j: (i, j)),
        compiler_params=_cparams(("parallel", "arbitrary")),
        name="matmul_res",
    )(a, w, res)


def _gdn_kernel(q_ref, k_ref, v_ref, z_ref, col_ref, row_ref, cwq_ref, cwk_ref, cwv_ref,
                alog_ref, dtb_ref, nw_ref, o_ref, state_ref, qbuf, kbuf, vbuf, *, tb):
    nchunk = tb // CHUNK

    @pl.when(pl.program_id(2) == 0)
    def _():
        state_ref[...] = jnp.zeros_like(state_ref)
        qbuf[0:HALO, :] = jnp.zeros((HALO, HEAD_DIM), F32)
        kbuf[0:HALO, :] = jnp.zeros((HALO, HEAD_DIM), F32)
        vbuf[0:HALO, :] = jnp.zeros((HALO, 2 * HEAD_DIM), F32)

    def conv_silu(x_ref, buf, cw_ref):
        buf[HALO:HALO + tb, :] = x_ref[...].astype(F32)
        acc = cw_ref[0:1, :] * buf[HALO - 3:HALO - 3 + tb, :]
        for kk in range(1, GDN_CONV):
            acc = acc + cw_ref[kk:kk + 1, :] * buf[HALO - 3 + kk:HALO - 3 + kk + tb, :]
        buf[0:HALO, :] = buf[tb:tb + HALO, :]
        return acc * _sigmoid(acc)

    def l2norm(x):
        return x * lax.rsqrt(jnp.sum(x * x, axis=-1, keepdims=True) + EPS)

    q = l2norm(conv_silu(q_ref, qbuf, cwq_ref)) * (HEAD_DIM ** -0.5)
    k = l2norm(conv_silu(k_ref, kbuf, cwk_ref))
    v = conv_silu(v_ref, vbuf, cwv_ref)

    neg_a_col = -jnp.exp(alog_ref[...])
    beta_col = _sigmoid(col_ref[:, 0:2])
    g_col = neg_a_col * _softplus(col_ref[:, 2:4] + dtb_ref[...])
    neg_a_row = -jnp.exp(alog_ref[...].reshape(2, 1))
    g_row = neg_a_row * _softplus(row_ref[2:4, :] + dtb_ref[...].reshape(2, 1))

    ri = lax.broadcasted_iota(jnp.int32, (tb, tb), 0)
    ci = lax.broadcasted_iota(jnp.int32, (tb, tb), 1)
    same_chunk = (ri // CHUNK) == (ci // CHUNK)
    blk_tril = jnp.where(same_chunk & (ci <= ri), 1.0, 0.0).astype(F32)
    blk_triu = jnp.where(same_chunk & (ri <= ci), 1.0, 0.0).astype(F32)
    gc_col = jnp.dot(blk_tril, g_col, precision=HIGHEST, preferred_element_type=F32)
    gc_row = jnp.dot(g_row, blk_triu, precision=HIGHEST, preferred_element_type=F32)

    r64 = lax.broadcasted_iota(jnp.int32, (CHUNK, CHUNK), 0)
    c64 = lax.broadcasted_iota(jnp.int32, (CHUNK, CHUNK), 1)
    tril = c64 <= r64
    strict = c64 < r64
    eye = jnp.where(c64 == r64, 1.0, 0.0).astype(F32)

    def mm(a, b):
        return jnp.dot(a.astype(BF16), b.astype(BF16), preferred_element_type=F32)

    def mm_nt(a, b):
        return lax.dot_general(a.astype(BF16), b.astype(BF16), (((1,), (1,)), ((), ())),
                               preferred_element_type=F32)

    def mm_tn(a, b):
        return lax.dot_general(a.astype(BF16), b.astype(BF16), (((0,), (0,)), ((), ())),
                               preferred_element_type=F32)

    def mm_hi(a, b):
        return jnp.dot(a, b, precision=HIGHEST, preferred_element_type=F32)

    nw = nw_ref[...]
    for c in range(nchunk):
        rows = slice(c * CHUNK, (c + 1) * CHUNK)
        qc, kc = q[rows], k[rows]
        kk_t = mm_nt(kc, kc)
        qk_t = mm_nt(qc, kc)
        for j in range(2):
            lanes = slice(j * HEAD_DIM, (j + 1) * HEAD_DIM)
            vc = v[rows, lanes]
            beta = beta_col[rows, j:j + 1]
            gcc = gc_col[rows, j:j + 1]
            gcr = gc_row[j:j + 1, rows]
            glast = gcc[CHUNK - 1:CHUNK, :]
            decay = jnp.exp(jnp.where(tril, gcc - gcr, -jnp.inf))
            lower = jnp.where(strict, beta * kk_t * decay, 0.0)
            attn = jnp.where(tril, qk_t * decay, 0.0)
            inv = eye - lower
            power = lower
            for _ in range(5):
                power = mm_hi(power, power)
                inv = inv + mm_hi(inv, power)
            egc = jnp.exp(gcc)
            u = mm(inv, vc * beta)
            w = mm(inv, kc * (beta * egc))
            state = state_ref[j]
            v_new = u - mm(w, state)
            o = mm(qc * egc, state) + mm(attn, v_new)
            k_dec = kc * jnp.exp(glast - gcc)
            state_ref[j] = state * jnp.exp(glast) + mm_tn(k_dec, v_new)
            zc = z_ref[rows, lanes].astype(F32)
            o = o * lax.rsqrt(jnp.mean(o * o, axis=-1, keepdims=True) + EPS) * nw
            o_ref[rows, lanes] = (o * (zc * _sigmoid(zc))).astype(o_ref.dtype)


def _gdn_core(proj, ba, conv_w, a_log, dt_bias, norm_w, *, batch, seq, tb):
    proj3 = proj.reshape(batch, seq, MAIN_DIM)
    ba4 = ba.reshape(batch, seq, 2, QK_HEADS, 2)
    cols = jnp.transpose(ba4, (0, 3, 1, 2, 4)).reshape(batch, QK_HEADS, seq, 4)
    rows = jnp.transpose(ba4, (0, 3, 2, 4, 1)).reshape(batch, QK_HEADS, 4, seq)
    alog2 = a_log.reshape(QK_HEADS, 1, 2)
    dtb2 = dt_bias.reshape(QK_HEADS, 1, 2)
    kq = KEY_DIM // HEAD_DIM
    vq = (2 * KEY_DIM) // (2 * HEAD_DIM)
    zq = QKV_DIM // (2 * HEAD_DIM)
    out = pl.pallas_call(
        functools.partial(_gdn_kernel, tb=tb),
        out_shape=jax.ShapeDtypeStruct((batch, seq, VAL_DIM), BF16),
        grid=(batch, QK_HEADS, seq // tb),
        in_specs=[
            pl.BlockSpec((None, tb, HEAD_DIM), lambda b, h, t: (b, t, h)),
            pl.BlockSpec((None, tb, HEAD_DIM), lambda b, h, t: (b, t, kq + h)),
            pl.BlockSpec((None, tb, 2 * HEAD_DIM), lambda b, h, t: (b, t, vq + h)),
            pl.BlockSpec((None, tb, 2 * HEAD_DIM), lambda b, h, t: (b, t, zq + h)),
            pl.BlockSpec((None, None, tb, 4), lambda b, h, t: (b, h, t, 0)),
            pl.BlockSpec((None, None, 4, tb), lambda b, h, t: (b, h, 0, t)),
            pl.BlockSpec((GDN_CONV, HEAD_DIM), lambda b, h, t: (0, h)),
            pl.BlockSpec((GDN_CONV, HEAD_DIM), lambda b, h, t: (0, kq + h)),
            pl.BlockSpec((GDN_CONV, 2 * HEAD_DIM), lambda b, h, t: (0, vq + h)),
            pl.BlockSpec((None, 1, 2), lambda b, h, t: (h, 0, 0)),
            pl.BlockSpec((None, 1, 2), lambda b, h, t: (h, 0, 0)),
            pl.BlockSpec((1, HEAD_DIM), lambda b, h, t: (0, 0)),
        ],
        out_specs=pl.BlockSpec((None, tb, 2 * HEAD_DIM), lambda b, h, t: (b, t, h)),
        scratch_shapes=[
            pltpu.VMEM((2, HEAD_DIM, HEAD_DIM), F32),
            pltpu.VMEM((tb + HALO, HEAD_DIM), F32),
            pltpu.VMEM((tb + HALO, HEAD_DIM), F32),
            pltpu.VMEM((tb + HALO, 2 * HEAD_DIM), F32),
        ],
        compiler_params=_cparams(("parallel", "parallel", "arbitrary")),
        name="gdn_core",
    )(proj3, proj3, proj3, proj3, cols, rows, conv_w, conv_w, conv_w, alog2, dtb2,
      norm_w.reshape(1, HEAD_DIM))
    return out.reshape(batch * seq, VAL_DIM)


def _sconv_kernel(b_ref, c_ref, h_ref, cw_ref, o_ref, buf, *, tm):
    @pl.when(pl.program_id(2) == 0)
    def _():
        buf[0:HALO, :] = jnp.zeros((HALO, buf.shape[1]), F32)

    buf[HALO:HALO + tm, :] = c_ref[...].astype(F32) * h_ref[...].astype(F32)
    off = HALO - (SC_WIDTH - 1)
    acc = cw_ref[0:1, :] * buf[off:off + tm, :]
    for kk in range(1, SC_WIDTH):
        acc = acc + cw_ref[kk:kk + 1, :] * buf[off + kk:off + kk + tm, :]
    buf[0:HALO, :] = buf[tm:tm + HALO, :]
    o_ref[...] = (b_ref[...].astype(F32) * acc).astype(o_ref.dtype)


def _sconv(proj, conv_w, *, batch, seq, tm, tn):
    d = conv_w.shape[1]
    proj3 = proj.reshape(batch, seq, 3 * d)
    nj = d // tn
    out = pl.pallas_call(
        functools.partial(_sconv_kernel, tm=tm),
        out_shape=jax.ShapeDtypeStruct((batch, seq, d), BF16),
        grid=(batch, nj, seq // tm),
        in_specs=[
            pl.BlockSpec((None, tm, tn), lambda b, j, t: (b, t, j)),
            pl.BlockSpec((None, tm, tn), lambda b, j, t: (b, t, nj + j)),
            pl.BlockSpec((None, tm, tn), lambda b, j, t: (b, t, 2 * nj + j)),
            pl.BlockSpec((SC_WIDTH, tn), lambda b, j, t: (0, j)),
        ],
        out_specs=pl.BlockSpec((None, tm, tn), lambda b, j, t: (b, t, j)),
        scratch_shapes=[pltpu.VMEM((tm + HALO, tn), F32)],
        compiler_params=_cparams(("parallel", "parallel", "arbitrary")),
        name="sconv",
    )(proj3, proj3, proj3, conv_w)
    return out.reshape(batch * seq, d)


def _router_kernel(h_ref, g_ref, wr_ref, br_ref, xn_ref, eid_ref, gate_ref):
    xn = _rms(h_ref[...], g_ref[...])
    xn_ref[...] = xn
    logits = jnp.dot(xn, wr_ref[...], precision=HIGHEST, preferred_element_type=F32) + br_ref[...]
    tm = logits.shape[0]
    glog = logits[:, 0:N_GROUPS]
    elog = logits[:, N_GROUPS:N_GROUPS + N_EXPERTS]
    gl = lax.broadcasted_iota(jnp.int32, (tm, N_GROUPS), 1)
    gmax = jnp.max(glog, axis=-1, keepdims=True)
    group = jnp.min(jnp.where(glog == gmax, gl, N_GROUPS), axis=-1, keepdims=True)
    p_group = 1.0 / jnp.sum(jnp.exp(glog - gmax), axis=-1, keepdims=True)
    el = lax.broadcasted_iota(jnp.int32, (tm, N_EXPERTS), 1)
    neg = jnp.float32(-jnp.inf)
    within = jnp.where((el // EXPERTS_PER_GROUP) == group, elog, neg)
    m1 = jnp.max(within, axis=-1, keepdims=True)
    i1 = jnp.min(jnp.where(within == m1, el, N_EXPERTS), axis=-1, keepdims=True)
    rest = jnp.where(el == i1, neg, within)
    m2 = jnp.max(rest, axis=-1, keepdims=True)
    i2 = jnp.min(jnp.where(rest == m2, el, N_EXPERTS), axis=-1, keepdims=True)
    e2 = jnp.exp(m2 - m1)
    g1 = p_group / (1.0 + e2)
    g2 = p_group * e2 / (1.0 + e2)
    lane2 = lax.broadcasted_iota(jnp.int32, (tm, 2), 1)
    eid_ref[...] = jnp.where(lane2 == 0, i1, i2)
    gate_ref[...] = jnp.where(lane2 == 0, g1, g2)


def _router(h, gain, w_router, b_router, *, tm):
    t, d = h.shape
    nr = w_router.shape[1]
    return pl.pallas_call(
        _router_kernel,
        out_shape=(jax.ShapeDtypeStruct((t, d), F32),
                   jax.ShapeDtypeStruct((t, 2), jnp.int32),
                   jax.ShapeDtypeStruct((t, 2), F32)),
        grid=(t // tm,),
        in_specs=[
            pl.BlockSpec((tm, d), lambda i: (i, 0)),
            pl.BlockSpec((1, d), lambda i: (0, 0)),
            pl.BlockSpec((d, nr), lambda i: (0, 0)),
            pl.BlockSpec((1, nr), lambda i: (0, 0)),
        ],
        out_specs=(pl.BlockSpec((tm, d), lambda i: (i, 0)),
                   pl.BlockSpec((tm, 2), lambda i: (i, 0)),
                   pl.BlockSpec((tm, 2), lambda i: (i, 0))),
        compiler_params=_cparams(("parallel",)),
        name="moe_router",
    )(h, gain.reshape(1, d), w_router, b_router.reshape(1, nr))


def _dispatch_kernel(d0_ref, d1_ref, x_ref, xs_in_ref, xs_ref, sem, *, tm):
    del xs_in_ref
    base = pl.program_id(0) * tm

    def row_copy(r, dest):
        return pltpu.make_async_copy(x_ref.at[pl.ds(r, 1), :], xs_ref.at[pl.ds(dest, 1), :], sem)

    def issue(r, carry):
        row_copy(r, d0_ref[base + r]).start()
        row_copy(r, d1_ref[base + r]).start()
        return carry

    lax.fori_loop(0, tm, issue, 0)

    def drain(r, carry):
        row_copy(r, 0).wait()
        row_copy(r, 0).wait()
        return carry

    lax.fori_loop(0, tm, drain, 0)


def _dispatch(xn, dest0, dest1, cap, *, tm):
    t, d = xn.shape
    xs0 = jnp.zeros((cap, d), F32)
    return pl.pallas_call(
        functools.partial(_dispatch_kernel, tm=tm),
        out_shape=jax.ShapeDtypeStruct((cap, d), F32),
        grid_spec=pltpu.PrefetchScalarGridSpec(
            num_scalar_prefetch=2,
            grid=(t // tm,),
            in_specs=[
                pl.BlockSpec((tm, d), lambda i, d0, d1: (i, 0)),
                pl.BlockSpec(memory_space=pl.ANY),
            ],
            out_specs=pl.BlockSpec(memory_space=pl.ANY),
            scratch_shapes=[pltpu.SemaphoreType.DMA],
        ),
        input_output_aliases={3: 0},
        compiler_params=_cparams(("arbitrary",)),
        name="moe_dispatch",
    )(dest0, dest1, xn, xs0)


def _expert_kernel(be_ref, nu_ref, x_ref, wgu_ref, wd_ref, o_ref):
    del be_ref
    used = pl.program_id(0) < nu_ref[0]

    @pl.when(used)
    def _():
        x = x_ref[...].astype(BF16)
        gu = jnp.dot(x, wgu_ref[...], preferred_element_type=F32)
        gt, up = gu[:, :D_EXPERT], gu[:, D_EXPERT:]
        act = (gt * _sigmoid(gt) * up).astype(BF16)
        o_ref[...] = jnp.dot(act, wd_ref[...], preferred_element_type=F32)

    @pl.when(jnp.logical_not(used))
    def _():
        o_ref[...] = jnp.zeros_like(o_ref)


def _experts(xs, w_gu, w_down, block_expert, n_used):
    cap, d = xs.shape
    nb = cap // MOE_BLOCK

    def blk(i, be, nu):
        return jnp.minimum(i, nu[0] - 1)

    return pl.pallas_call(
        _expert_kernel,
        out_shape=jax.ShapeDtypeStruct((cap, d), F32),
        grid_spec=pltpu.PrefetchScalarGridSpec(
            num_scalar_prefetch=2,
            grid=(nb,),
            in_specs=[
                pl.BlockSpec((MOE_BLOCK, d), lambda i, be, nu: (blk(i, be, nu), 0)),
                pl.BlockSpec((None, d, 2 * D_EXPERT), lambda i, be, nu: (be[blk(i, be, nu)], 0, 0)),
                pl.BlockSpec((None, D_EXPERT, d), lambda i, be, nu: (be[blk(i, be, nu)], 0, 0)),
            ],
            out_specs=pl.BlockSpec((MOE_BLOCK, d), lambda i, be, nu: (i, 0)),
        ),
        compiler_params=_cparams(("arbitrary",)),
        name="moe_experts",
    )(block_expert, n_used, xs, w_gu, w_down)


def _combine_kernel(d0_ref, d1_ref, h_ref, gate_ref, fg_ref, ys_ref, o_ref, buf, sem, *, tm, final):
    base = pl.program_id(0) * tm

    def row_copy(r, k, src):
        return pltpu.make_async_copy(ys_ref.at[pl.ds(src, 1), :], buf.at[k, pl.ds(r, 1), :], sem)

    def issue(r, carry):
        row_copy(r, 0, d0_ref[base + r]).start()
        row_copy(r, 1, d1_ref[base + r]).start()
        return carry

    lax.fori_loop(0, tm, issue, 0)

    def drain(r, carry):
        row_copy(r, 0, 0).wait()
        row_copy(r, 1, 0).wait()
        return carry

    lax.fori_loop(0, tm, drain, 0)
    gate = gate_ref[...]
    y = h_ref[...] + (gate[:, 0:1] * buf[0] + gate[:, 1:2] * buf[1])
    if final:
        y = _rms(y, fg_ref[...])
    o_ref[...] = y


def _combine(h, gates, dest0, dest1, ys, final_gain, *, tm, final):
    t, d = h.shape
    return pl.pallas_call(
        functools.partial(_combine_kernel, tm=tm, final=final),
        out_shape=jax.ShapeDtypeStruct((t, d), F32),
        grid_spec=pltpu.PrefetchScalarGridSpec(
            num_scalar_prefetch=2,
            grid=(t // tm,),
            in_specs=[
                pl.BlockSpec((tm, d), lambda i, d0, d1: (i, 0)),
                pl.BlockSpec((tm, 2), lambda i, d0, d1: (i, 0)),
                pl.BlockSpec((1, d), lambda i, d0, d1: (0, 0)),
                pl.BlockSpec(memory_space=pl.ANY),
            ],
            out_specs=pl.BlockSpec((tm, d), lambda i, d0, d1: (i, 0)),
            scratch_shapes=[pltpu.VMEM((2, tm, d), F32), pltpu.SemaphoreType.DMA],
        ),
        compiler_params=_cparams(("arbitrary",)),
        name="moe_combine",
    )(dest0, dest1, h, gates, final_gain.reshape(1, d), ys)


def _slot_table(eid, t):
    onehot = (eid[:, :, None] == jnp.arange(N_EXPERTS, dtype=jnp.int32)).astype(jnp.int32)
    per_tok = onehot[:, 0, :] + onehot[:, 1, :]
    csum = jnp.cumsum(per_tok, axis=0)
    counts = csum[-1]
    rank = jnp.take_along_axis(csum - per_tok, eid, axis=1)
    padded = (counts + MOE_BLOCK - 1) // MOE_BLOCK * MOE_BLOCK
    pend = jnp.cumsum(padded)
    pstart = pend - padded
    dest = pstart[eid] + rank
    n_blocks = (2 * t + MOE_BLOCK - 1) // MOE_BLOCK + N_EXPERTS
    block_start = jnp.arange(n_blocks, dtype=jnp.int32) * MOE_BLOCK
    block_expert = jnp.minimum(jnp.searchsorted(pend, block_start, side="right"),
                               N_EXPERTS - 1).astype(jnp.int32)
    n_used = (pend[-1] // MOE_BLOCK).astype(jnp.int32).reshape(1)
    return dest.astype(jnp.int32), block_expert, n_used, n_blocks * MOE_BLOCK


def _moe(h, gain, w_group, b_group, w_expert, b_expert, w_gu, w_down, final_gain, *, final, tm):
    t, _ = h.shape
    w_router = jnp.concatenate([w_group, w_expert], axis=1)
    b_router = jnp.concatenate([b_group, b_expert], axis=0)
    xn, eid, gates = _router(h, gain, w_router, b_router, tm=tm)
    dest, block_expert, n_used, cap = _slot_table(eid, t)
    dest0, dest1 = dest[:, 0], dest[:, 1]
    xs = _dispatch(xn, dest0, dest1, cap, tm=tm)
    ys = _experts(xs, w_gu.astype(BF16), w_down.astype(BF16), block_expert, n_used)
    return _combine(h, gates, dest0, dest1, ys, final_gain, tm=tm, final=final)


def _pick(n, pref):
    for c in pref:
        if n % c == 0:
            return c
    return n


def kernel(x, norm_mix, norm_ffn, gdn_w_in, gdn_conv_w, gdn_a_log, gdn_dt_bias, gdn_norm_w,
           gdn_w_out, sc_w_in, sc_conv_w, sc_w_out, moe_w_group, moe_b_group, moe_w_expert,
           moe_b_expert, moe_w_gu, moe_w_down, norm_final):
    batch, seq, d = x.shape
    t = batch * seq
    h = x.reshape(t, d)
    tm = _pick(t, (1024, 512, 256))
    tb = _pick(seq, (256, 128, 64))

    w_in = gdn_w_in[0]
    proj = _norm_matmul(h, norm_mix[0], w_in[:, :MAIN_DIM].astype(BF16), tm=tm, tn=1024,
                        out_dtype=F32)
    ba = _norm_side(h, norm_mix[0], w_in[:, MAIN_DIM:], tm=tm)
    o = _gdn_core(proj, ba, gdn_conv_w[0], gdn_a_log[0], gdn_dt_bias[0], gdn_norm_w[0],
                  batch=batch, seq=seq, tb=tb)
    h = _matmul_res(o, gdn_w_out[0].astype(BF16), h, tm=tm, tn=1024)
    h = _moe(h, norm_ffn[0], moe_w_group[0], moe_b_group[0], moe_w_expert[0], moe_b_expert[0],
             moe_w_gu[0], moe_w_down[0], norm_final, final=False, tm=_pick(t, (256,)))

    proj = _norm_matmul(h, norm_mix[1], sc_w_in[0].astype(BF16), tm=tm, tn=1024, out_dtype=F32)
    y = _sconv(proj, sc_conv_w[0], batch=batch, seq=seq, tm=_pick(seq, (512, 256)), tn=512)
    h = _matmul_res(y, sc_w_out[0].astype(BF16), h, tm=tm, tn=1024)
    h = _moe(h, norm_ffn[1], moe_w_group[1], moe_b_group[1], moe_w_expert[1], moe_b_expert[1],
             moe_w_gu[1], moe_w_down[1], norm_final, final=True, tm=_pick(t, (256,)))
    return h.reshape(batch, seq, d)
```

```python
import functools

import jax
import jax.numpy as jnp
from jax import lax
from jax.experimental import pallas as pl
from jax.experimental.pallas import tpu as pltpu

EPS = 1e-6
F32 = jnp.float32
BF16 = jnp.bfloat16
HIGHEST = lax.Precision.HIGHEST

QK_HEADS = 16
V_HEADS = 32
HEAD_DIM = 128
KEY_DIM = QK_HEADS * HEAD_DIM
VAL_DIM = V_HEADS * HEAD_DIM
QKV_DIM = 2 * KEY_DIM + VAL_DIM
MAIN_DIM = QKV_DIM + VAL_DIM
GDN_CONV = 4
CHUNK = 64
HALO = 8
GDN_HEADS_PER_STEP = 2

SC_WIDTH = 3

N_GROUPS = 8
EXPERTS_PER_GROUP = 8
N_EXPERTS = N_GROUPS * EXPERTS_PER_GROUP
D_EXPERT = 512
MOE_BLOCK = 256
RANK_BITS = 16

VMEM_LIMIT = 56 * 1024 * 1024


def _cparams(sem):
    return pltpu.CompilerParams(dimension_semantics=sem, vmem_limit_bytes=VMEM_LIMIT)


def _rms(x, gain):
    return x * lax.rsqrt(jnp.mean(x * x, axis=-1, keepdims=True) + EPS) * gain


def _sigmoid(x):
    return 1.0 / (1.0 + jnp.exp(-x))


def _softplus(x):
    return jnp.maximum(x, 0.0) + jnp.log1p(jnp.exp(-jnp.abs(x)))


def _norm_matmul_kernel(x_ref, g_ref, w_ref, o_ref, xn_ref):
    @pl.when(pl.program_id(1) == 0)
    def _():
        xn_ref[...] = _rms(x_ref[...], g_ref[...]).astype(BF16)

    o_ref[...] = jnp.dot(xn_ref[...], w_ref[...], preferred_element_type=F32).astype(o_ref.dtype)


def _norm_matmul(x, gain, w, *, tm, tn, out_dtype):
    t, k = x.shape
    n = w.shape[1]
    return pl.pallas_call(
        _norm_matmul_kernel,
        out_shape=jax.ShapeDtypeStruct((t, n), out_dtype),
        grid=(t // tm, n // tn),
        in_specs=[
            pl.BlockSpec((tm, k), lambda i, j: (i, 0)),
            pl.BlockSpec((1, k), lambda i, j: (0, 0)),
            pl.BlockSpec((k, tn), lambda i, j: (0, j)),
        ],
        out_specs=pl.BlockSpec((tm, tn), lambda i, j: (i, j)),
        scratch_shapes=[pltpu.VMEM((tm, k), BF16)],
        compiler_params=_cparams(("parallel", "arbitrary")),
        name="norm_matmul",
    )(x, gain.reshape(1, k), w)


def _norm_side_kernel(x_ref, g_ref, w_ref, o_ref):
    xn = _rms(x_ref[...], g_ref[...])
    o_ref[...] = jnp.dot(xn, w_ref[...], precision=HIGHEST, preferred_element_type=F32)


def _norm_side(x, gain, w, *, tm):
    t, k = x.shape
    n = w.shape[1]
    return pl.pallas_call(
        _norm_side_kernel,
        out_shape=jax.ShapeDtypeStruct((t, n), F32),
        grid=(t // tm,),
        in_specs=[
            pl.BlockSpec((tm, k), lambda i: (i, 0)),
            pl.BlockSpec((1, k), lambda i: (0, 0)),
            pl.BlockSpec((k, n), lambda i: (0, 0)),
        ],
        out_specs=pl.BlockSpec((tm, n), lambda i: (i, 0)),
        compiler_params=_cparams(("parallel",)),
        name="norm_side",
    )(x, gain.reshape(1, k), w)


def _matmul_res_kernel(a_ref, w_ref, r_ref, o_ref):
    o_ref[...] = r_ref[...] + jnp.dot(a_ref[...], w_ref[...], preferred_element_type=F32)


def _matmul_res(a, w, res, *, tm, tn):
    t, k = a.shape
    n = w.shape[1]
    return pl.pallas_call(
        _matmul_res_kernel,
        out_shape=jax.ShapeDtypeStruct((t, n), F32),
        grid=(t // tm, n // tn),
        in_specs=[
            pl.BlockSpec((tm, k), lambda i, j: (i, 0)),
            pl.BlockSpec((k, tn), lambda i, j: (0, j)),
            pl.BlockSpec((tm, tn), lambda i, j: (i, j)),
        ],
        out_specs=pl.BlockSpec((tm, tn), lambda i, j: (i, j)),
        compiler_params=_cparams(("parallel", "arbitrary")),
        name="matmul_res",
    )(a, w, res)


def _gdn_kernel(q_ref, k_ref, v_ref, z_ref, col_ref, row_ref, cwq_ref, cwk_ref, cwv_ref,
                alog_ref, dtb_ref, nw_ref, o_ref, state_ref, qbuf, kbuf, vbuf, *, tb, hp):
    nchunk = tb // CHUNK

    @pl.when(pl.program_id(2) == 0)
    def _():
        state_ref[...] = jnp.zeros_like(state_ref)
        qbuf[0:HALO, :] = jnp.zeros((HALO, qbuf.shape[1]), F32)
        kbuf[0:HALO, :] = jnp.zeros((HALO, kbuf.shape[1]), F32)
        vbuf[0:HALO, :] = jnp.zeros((HALO, vbuf.shape[1]), F32)

    def conv_silu(x_ref, buf, cw_ref):
        buf[HALO:HALO + tb, :] = x_ref[...].astype(F32)
        acc = cw_ref[0:1, :] * buf[HALO - 3:HALO - 3 + tb, :]
        for kk in range(1, GDN_CONV):
            acc = acc + cw_ref[kk:kk + 1, :] * buf[HALO - 3 + kk:HALO - 3 + kk + tb, :]
        buf[0:HALO, :] = buf[tb:tb + HALO, :]
        return acc * _sigmoid(acc)

    def l2norm(x):
        return x * lax.rsqrt(jnp.sum(x * x, axis=-1, keepdims=True) + EPS)

    q_all = conv_silu(q_ref, qbuf, cwq_ref)
    k_all = conv_silu(k_ref, kbuf, cwk_ref)
    v_all = conv_silu(v_ref, vbuf, cwv_ref)

    ri = lax.broadcasted_iota(jnp.int32, (tb, tb), 0)
    ci = lax.broadcasted_iota(jnp.int32, (tb, tb), 1)
    same_chunk = (ri // CHUNK) == (ci // CHUNK)
    blk_tril = jnp.where(same_chunk & (ci <= ri), 1.0, 0.0).astype(BF16)
    blk_triu = jnp.where(same_chunk & (ri <= ci), 1.0, 0.0).astype(BF16)

    def split3(x, axis):
        hi = x.astype(BF16).astype(F32)
        r1 = x - hi
        mid = r1.astype(BF16).astype(F32)
        lo = r1 - mid
        return jnp.concatenate([hi, mid, lo], axis=axis).astype(BF16)

    r64 = lax.broadcasted_iota(jnp.int32, (CHUNK, CHUNK), 0)
    c64 = lax.broadcasted_iota(jnp.int32, (CHUNK, CHUNK), 1)
    tril = c64 <= r64
    strict = c64 < r64
    eye = jnp.where(c64 == r64, 1.0, 0.0).astype(F32)

    def mm(a, b):
        return jnp.dot(a.astype(BF16), b.astype(BF16), preferred_element_type=F32)

    def mm_nt(a, b):
        return lax.dot_general(a.astype(BF16), b.astype(BF16), (((1,), (1,)), ((), ())),
                               preferred_element_type=F32)

    def mm_tn(a, b):
        return lax.dot_general(a.astype(BF16), b.astype(BF16), (((0,), (0,)), ((), ())),
                               preferred_element_type=F32)

    chunks = range(nchunk)
    rows = [slice(c * CHUNK, (c + 1) * CHUNK) for c in chunks]
    pairs = [(p, c) for p in range(hp) for c in chunks]
    heads = [(p, c, j) for p in range(hp) for c in chunks for j in range(2)]

    qc, kc, beta_col, gc_col, egc_col, gc_row = {}, {}, {}, {}, {}, {}
    for p in range(hp):
        hl = slice(p * HEAD_DIM, (p + 1) * HEAD_DIM)
        q = l2norm(q_all[:, hl]) * (HEAD_DIM ** -0.5)
        k = l2norm(k_all[:, hl])
        for c in chunks:
            qc[p, c], kc[p, c] = q[rows[c]], k[rows[c]]
        alog, dtb = alog_ref[p], dtb_ref[p]
        beta_col[p] = _sigmoid(col_ref[p, :, 0:2])
        g_col = -jnp.exp(alog) * _softplus(col_ref[p, :, 2:4] + dtb)
        g_row = -jnp.exp(alog.reshape(2, 1)) * _softplus(row_ref[p, 2:4, :] + dtb.reshape(2, 1))
        c3 = jnp.dot(blk_tril, split3(g_col, 1), preferred_element_type=F32)
        gc_col[p] = (c3[:, 0:2] + c3[:, 2:4]) + c3[:, 4:6]
        r3 = jnp.dot(split3(g_row, 0), blk_triu, preferred_element_type=F32)
        gc_row[p] = (r3[0:2, :] + r3[2:4, :]) + r3[4:6, :]
        egc_col[p] = jnp.exp(gc_col[p])

    kk_t = {pc: mm_nt(kc[pc], kc[pc]) for pc in pairs}
    qk_t = {pc: mm_nt(qc[pc], kc[pc]) for pc in pairs}

    beta, gcc, egc, glast, lower, attn = {}, {}, {}, {}, {}, {}
    for h in heads:
        p, c, j = h
        beta[h] = beta_col[p][rows[c], j:j + 1]
        gcc[h] = gc_col[p][rows[c], j:j + 1]
        egc[h] = egc_col[p][rows[c], j:j + 1]
        glast[h] = gcc[h][CHUNK - 1:CHUNK, :]
        gcr = gc_row[p][j:j + 1, rows[c]]
        decay = jnp.exp(jnp.where(tril, gcc[h] - gcr, -jnp.inf))
        lower[h] = jnp.where(strict, beta[h] * kk_t[p, c] * decay, 0.0)
        attn[h] = jnp.where(tril, qk_t[p, c] * decay, 0.0)

    inv = {h: eye - lower[h] for h in heads}
    power = dict(lower)
    for _ in range(5):
        power = {h: mm(power[h], power[h]) for h in heads}
        inv = {h: inv[h] + mm(inv[h], power[h]) for h in heads}

    uw = {}
    for h in heads:
        p, c, j = h
        vl = slice((2 * p + j) * HEAD_DIM, (2 * p + j + 1) * HEAD_DIM)
        rhs = jnp.concatenate([v_all[rows[c], vl] * beta[h], kc[p, c] * (beta[h] * egc[h])], axis=1)
        uw[h] = mm(inv[h], rhs)
    nk = {h: mm_tn(kc[h[0], h[1]] * jnp.exp(glast[h] - gcc[h]), uw[h]) for h in heads}
    ao = {h: mm(attn[h], uw[h]) for h in heads}
    qp = {h: (qc[h[0], h[1]] * egc[h] - ao[h][:, HEAD_DIM:]).astype(BF16) for h in heads}

    state = {(p, j): state_ref[2 * p + j] for p in range(hp) for j in range(2)}
    seen = {}
    for c in chunks:
        for p in range(hp):
            for j in range(2):
                h = (p, c, j)
                s_bf = state[p, j].astype(BF16)
                seen[h] = s_bf
                n_c, k_c = nk[h][:, :HEAD_DIM], nk[h][:, HEAD_DIM:]
                state[p, j] = state[p, j] * jnp.exp(glast[h]) + (n_c - mm(k_c, s_bf))
    for p in range(hp):
        for j in range(2):
            state_ref[2 * p + j] = state[p, j]

    nw = nw_ref[...]
    for h in heads:
        p, c, j = h
        vl = slice((2 * p + j) * HEAD_DIM, (2 * p + j + 1) * HEAD_DIM)
        o = jnp.dot(qp[h], seen[h], preferred_element_type=F32) + ao[h][:, :HEAD_DIM]
        zc = z_ref[rows[c], vl].astype(F32)
        o = o * lax.rsqrt(jnp.mean(o * o, axis=-1, keepdims=True) + EPS) * nw
        o_ref[rows[c], vl] = (o * (zc * _sigmoid(zc))).astype(o_ref.dtype)


def _gdn_core(proj, ba, conv_w, a_log, dt_bias, norm_w, *, batch, seq, tb, hp):
    proj3 = proj.reshape(batch, seq, MAIN_DIM)
    ba4 = ba.reshape(batch, seq, 2, QK_HEADS, 2)
    cols = jnp.transpose(ba4, (0, 3, 1, 2, 4)).reshape(batch, QK_HEADS, seq, 4)
    rows = jnp.transpose(ba4, (0, 3, 2, 4, 1)).reshape(batch, QK_HEADS, 4, seq)
    alog2 = a_log.reshape(QK_HEADS, 1, 2)
    dtb2 = dt_bias.reshape(QK_HEADS, 1, 2)
    qw, vw = hp * HEAD_DIM, 2 * hp * HEAD_DIM
    kq = KEY_DIM // qw
    vq = (2 * KEY_DIM) // vw
    zq = QKV_DIM // vw
    out = pl.pallas_call(
        functools.partial(_gdn_kernel, tb=tb, hp=hp),
        out_shape=jax.ShapeDtypeStruct((batch, seq, VAL_DIM), BF16),
        grid=(batch, QK_HEADS // hp, seq // tb),
        in_specs=[
            pl.BlockSpec((None, tb, qw), lambda b, h, t: (b, t, h)),
            pl.BlockSpec((None, tb, qw), lambda b, h, t: (b, t, kq + h)),
            pl.BlockSpec((None, tb, vw), lambda b, h, t: (b, t, vq + h)),
            pl.BlockSpec((None, tb, vw), lambda b, h, t: (b, t, zq + h)),
            pl.BlockSpec((None, hp, tb, 4), lambda b, h, t: (b, h, t, 0)),
            pl.BlockSpec((None, hp, 4, tb), lambda b, h, t: (b, h, 0, t)),
            pl.BlockSpec((GDN_CONV, qw), lambda b, h, t: (0, h)),
            pl.BlockSpec((GDN_CONV, qw), lambda b, h, t: (0, kq + h)),
            pl.BlockSpec((GDN_CONV, vw), lambda b, h, t: (0, vq + h)),
            pl.BlockSpec((hp, 1, 2), lambda b, h, t: (h, 0, 0)),
            pl.BlockSpec((hp, 1, 2), lambda b, h, t: (h, 0, 0)),
            pl.BlockSpec((1, HEAD_DIM), lambda b, h, t: (0, 0)),
        ],
        out_specs=pl.BlockSpec((None, tb, vw), lambda b, h, t: (b, t, h)),
        scratch_shapes=[
            pltpu.VMEM((2 * hp, HEAD_DIM, HEAD_DIM), F32),
            pltpu.VMEM((tb + HALO, qw), F32),
            pltpu.VMEM((tb + HALO, qw), F32),
            pltpu.VMEM((tb + HALO, vw), F32),
        ],
        compiler_params=_cparams(("parallel", "parallel", "arbitrary")),
        name="gdn_core",
    )(proj3, proj3, proj3, proj3, cols, rows, conv_w, conv_w, conv_w, alog2, dtb2,
      norm_w.reshape(1, HEAD_DIM))
    return out.reshape(batch * seq, VAL_DIM)


def _sconv_kernel(b_ref, c_ref, h_ref, cw_ref, o_ref, buf, *, tm):
    @pl.when(pl.program_id(2) == 0)
    def _():
        buf[0:HALO, :] = jnp.zeros((HALO, buf.shape[1]), F32)

    buf[HALO:HALO + tm, :] = c_ref[...].astype(F32) * h_ref[...].astype(F32)
    off = HALO - (SC_WIDTH - 1)
    acc = cw_ref[0:1, :] * buf[off:off + tm, :]
    for kk in range(1, SC_WIDTH):
        acc = acc + cw_ref[kk:kk + 1, :] * buf[off + kk:off + kk + tm, :]
    buf[0:HALO, :] = buf[tm:tm + HALO, :]
    o_ref[...] = (b_ref[...].astype(F32) * acc).astype(o_ref.dtype)


def _sconv(proj, conv_w, *, batch, seq, tm, tn):
    d = conv_w.shape[1]
    proj3 = proj.reshape(batch, seq, 3 * d)
    nj = d // tn
    out = pl.pallas_call(
        functools.partial(_sconv_kernel, tm=tm),
        out_shape=jax.ShapeDtypeStruct((batch, seq, d), BF16),
        grid=(batch, nj, seq // tm),
        in_specs=[
            pl.BlockSpec((None, tm, tn), lambda b, j, t: (b, t, j)),
            pl.BlockSpec((None, tm, tn), lambda b, j, t: (b, t, nj + j)),
            pl.BlockSpec((None, tm, tn), lambda b, j, t: (b, t, 2 * nj + j)),
            pl.BlockSpec((SC_WIDTH, tn), lambda b, j, t: (0, j)),
        ],
        out_specs=pl.BlockSpec((None, tm, tn), lambda b, j, t: (b, t, j)),
        scratch_shapes=[pltpu.VMEM((tm + HALO, tn), F32)],
        compiler_params=_cparams(("parallel", "parallel", "arbitrary")),
        name="sconv",
    )(proj3, proj3, proj3, conv_w)
    return out.reshape(batch * seq, d)


def _router_kernel(h_ref, g_ref, wr_ref, br_ref, xn_ref, code_ref, gate_ref, cnt_ref, carry_ref):
    @pl.when(pl.program_id(0) == 0)
    def _():
        carry_ref[...] = jnp.zeros_like(carry_ref)

    xn = _rms(h_ref[...], g_ref[...])
    xn_ref[...] = xn
    logits = jnp.dot(xn, wr_ref[...], precision=HIGHEST, preferred_element_type=F32) + br_ref[...]
    tm = logits.shape[0]
    glog = logits[:, 0:N_GROUPS]
    elog = logits[:, N_GROUPS:N_GROUPS + N_EXPERTS]
    gl = lax.broadcasted_iota(jnp.int32, (tm, N_GROUPS), 1)
    gmax = jnp.max(glog, axis=-1, keepdims=True)
    group = jnp.min(jnp.where(glog == gmax, gl, N_GROUPS), axis=-1, keepdims=True)
    p_group = 1.0 / jnp.sum(jnp.exp(glog - gmax), axis=-1, keepdims=True)
    el = lax.broadcasted_iota(jnp.int32, (tm, N_EXPERTS), 1)
    neg = jnp.float32(-jnp.inf)
    within = jnp.where((el // EXPERTS_PER_GROUP) == group, elog, neg)
    m1 = jnp.max(within, axis=-1, keepdims=True)
    i1 = jnp.min(jnp.where(within == m1, el, N_EXPERTS), axis=-1, keepdims=True)
    rest = jnp.where(el == i1, neg, within)
    m2 = jnp.max(rest, axis=-1, keepdims=True)
    i2 = jnp.min(jnp.where(rest == m2, el, N_EXPERTS), axis=-1, keepdims=True)
    e2 = jnp.exp(m2 - m1)
    g1 = p_group / (1.0 + e2)
    g2 = p_group * e2 / (1.0 + e2)

    pick1, pick2 = el == i1, el == i2
    onehot = jnp.where(pick1 | pick2, 1.0, 0.0)
    rr = lax.broadcasted_iota(jnp.int32, (tm, tm), 0)
    cc = lax.broadcasted_iota(jnp.int32, (tm, tm), 1)
    before = jnp.where(cc < rr, 1.0, 0.0).astype(BF16)
    prefix = jnp.dot(before, onehot.astype(BF16), preferred_element_type=F32) + carry_ref[...]
    r1 = jnp.sum(jnp.where(pick1, prefix, 0.0), axis=-1, keepdims=True).astype(jnp.int32)
    r2 = jnp.sum(jnp.where(pick2, prefix, 0.0), axis=-1, keepdims=True).astype(jnp.int32)
    carry_ref[...] = carry_ref[...] + jnp.sum(onehot, axis=0, keepdims=True)
    cnt_ref[...] = carry_ref[...]

    lane2 = lax.broadcasted_iota(jnp.int32, (tm, 2), 1)
    code1 = jnp.left_shift(i1, RANK_BITS) | r1
    code2 = jnp.left_shift(i2, RANK_BITS) | r2
    code_ref[...] = jnp.where(lane2 == 0, code1, code2)
    gate_ref[...] = jnp.where(lane2 == 0, g1, g2)


def _router(h, gain, w_router, b_router, *, tm):
    t, d = h.shape
    nr = w_router.shape[1]
    return pl.pallas_call(
        _router_kernel,
        out_shape=(jax.ShapeDtypeStruct((t, d), F32),
                   jax.ShapeDtypeStruct((t, 2), jnp.int32),
                   jax.ShapeDtypeStruct((t, 2), F32),
                   jax.ShapeDtypeStruct((1, N_EXPERTS), F32)),
        grid=(t // tm,),
        in_specs=[
            pl.BlockSpec((tm, d), lambda i: (i, 0)),
            pl.BlockSpec((1, d), lambda i: (0, 0)),
            pl.BlockSpec((d, nr), lambda i: (0, 0)),
            pl.BlockSpec((1, nr), lambda i: (0, 0)),
        ],
        out_specs=(pl.BlockSpec((tm, d), lambda i: (i, 0)),
                   pl.BlockSpec((tm, 2), lambda i: (i, 0)),
                   pl.BlockSpec((tm, 2), lambda i: (i, 0)),
                   pl.BlockSpec((1, N_EXPERTS), lambda i: (0, 0))),
        scratch_shapes=[pltpu.VMEM((1, N_EXPERTS), F32)],
        compiler_params=_cparams(("arbitrary",)),
        name="moe_router",
    )(h, gain.reshape(1, d), w_router, b_router.reshape(1, nr))


def _slot(pstart_ref, code):
    return pstart_ref[lax.shift_right_logical(code, RANK_BITS)] + (code & ((1 << RANK_BITS) - 1))


def _dispatch_kernel(ps_ref, c0_ref, c1_ref, x_ref, xs_in_ref, xs_ref, sem, *, tm):
    del xs_in_ref
    base = pl.program_id(0) * tm

    def row_copy(r, dest):
        return pltpu.make_async_copy(x_ref.at[pl.ds(r, 1), :], xs_ref.at[pl.ds(dest, 1), :], sem)

    def issue(r, carry):
        row_copy(r, _slot(ps_ref, c0_ref[base + r])).start()
        row_copy(r, _slot(ps_ref, c1_ref[base + r])).start()
        return carry

    lax.fori_loop(0, tm, issue, 0, unroll=8)
    whole = pltpu.make_async_copy(x_ref, xs_ref.at[pl.ds(0, tm), :], sem)
    whole.wait()
    whole.wait()


def _dispatch(xn, pstart, code0, code1, cap, *, tm):
    t, d = xn.shape
    xs0 = jnp.zeros((cap, d), F32)
    return pl.pallas_call(
        functools.partial(_dispatch_kernel, tm=tm),
        out_shape=jax.ShapeDtypeStruct((cap, d), F32),
        grid_spec=pltpu.PrefetchScalarGridSpec(
            num_scalar_prefetch=3,
            grid=(t // tm,),
            in_specs=[
                pl.BlockSpec((tm, d), lambda i, ps, c0, c1: (i, 0)),
                pl.BlockSpec(memory_space=pl.ANY),
            ],
            out_specs=pl.BlockSpec(memory_space=pl.ANY),
            scratch_shapes=[pltpu.SemaphoreType.DMA],
        ),
        input_output_aliases={4: 0},
        compiler_params=_cparams(("arbitrary",)),
        name="moe_dispatch",
    )(pstart, code0, code1, xn, xs0)


def _expert_kernel(be_ref, nu_ref, x_ref, wgu_ref, wd_ref, o_ref, wgu_bf, wd_bf):
    i = pl.program_id(0)
    used = i < nu_ref[0]
    new_expert = (i == 0) | (be_ref[i] != be_ref[jnp.maximum(i - 1, 0)])

    @pl.when(used & new_expert)
    def _():
        wgu_bf[...] = wgu_ref[...].astype(BF16)
        wd_bf[...] = wd_ref[...].astype(BF16)

    @pl.when(used)
    def _():
        x = x_ref[...].astype(BF16)
        gu = jnp.dot(x, wgu_bf[...], preferred_element_type=F32)
        gt, up = gu[:, :D_EXPERT], gu[:, D_EXPERT:]
        act = (gt * _sigmoid(gt) * up).astype(BF16)
        o_ref[...] = jnp.dot(act, wd_bf[...], preferred_element_type=F32)

    @pl.when(jnp.logical_not(used))
    def _():
        o_ref[...] = jnp.zeros_like(o_ref)


def _experts(xs, w_gu, w_down, block_expert, n_used):
    cap, d = xs.shape
    nb = cap // MOE_BLOCK

    def blk(i, be, nu):
        return jnp.minimum(i, nu[0] - 1)

    return pl.pallas_call(
        _expert_kernel,
        out_shape=jax.ShapeDtypeStruct((cap, d), F32),
        grid_spec=pltpu.PrefetchScalarGridSpec(
            num_scalar_prefetch=2,
            grid=(nb,),
            in_specs=[
                pl.BlockSpec((MOE_BLOCK, d), lambda i, be, nu: (blk(i, be, nu), 0)),
                pl.BlockSpec((None, d, 2 * D_EXPERT), lambda i, be, nu: (be[blk(i, be, nu)], 0, 0)),
                pl.BlockSpec((None, D_EXPERT, d), lambda i, be, nu: (be[blk(i, be, nu)], 0, 0)),
            ],
            out_specs=pl.BlockSpec((MOE_BLOCK, d), lambda i, be, nu: (i, 0)),
            scratch_shapes=[pltpu.VMEM((d, 2 * D_EXPERT), BF16), pltpu.VMEM((D_EXPERT, d), BF16)],
        ),
        compiler_params=_cparams(("arbitrary",)),
        name="moe_experts",
    )(block_expert, n_used, xs, w_gu, w_down)


def _combine_kernel(ps_ref, c0_ref, c1_ref, h_ref, gate_ref, fg_ref, ys_ref, o_ref, buf, sem, *,
                    tm, final):
    base = pl.program_id(0) * tm

    def row_copy(r, k, src):
        return pltpu.make_async_copy(ys_ref.at[pl.ds(src, 1), :], buf.at[k, pl.ds(r, 1), :], sem)

    def issue(r, carry):
        row_copy(r, 0, _slot(ps_ref, c0_ref[base + r])).start()
        row_copy(r, 1, _slot(ps_ref, c1_ref[base + r])).start()
        return carry

    lax.fori_loop(0, tm, issue, 0, unroll=8)
    for k in range(2):
        pltpu.make_async_copy(ys_ref.at[pl.ds(0, tm), :], buf.at[k], sem).wait()
    gate = gate_ref[...]
    y = h_ref[...] + (gate[:, 0:1] * buf[0] + gate[:, 1:2] * buf[1])
    if final:
        y = _rms(y, fg_ref[...])
    o_ref[...] = y


def _combine(h, gates, pstart, code0, code1, ys, final_gain, *, tm, final):
    t, d = h.shape
    return pl.pallas_call(
        functools.partial(_combine_kernel, tm=tm, final=final),
        out_shape=jax.ShapeDtypeStruct((t, d), F32),
        grid_spec=pltpu.PrefetchScalarGridSpec(
            num_scalar_prefetch=3,
            grid=(t // tm,),
            in_specs=[
                pl.BlockSpec((tm, d), lambda i, ps, c0, c1: (i, 0)),
                pl.BlockSpec((tm, 2), lambda i, ps, c0, c1: (i, 0)),
                pl.BlockSpec((1, d), lambda i, ps, c0, c1: (0, 0)),
                pl.BlockSpec(memory_space=pl.ANY),
            ],
            out_specs=pl.BlockSpec((tm, d), lambda i, ps, c0, c1: (i, 0)),
            scratch_shapes=[pltpu.VMEM((2, tm, d), F32), pltpu.SemaphoreType.DMA],
        ),
        compiler_params=_cparams(("arbitrary",)),
        name="moe_combine",
    )(pstart, code0, code1, h, gates, final_gain.reshape(1, d), ys)


def _block_table(counts, t):
    counts = counts.astype(jnp.int32)
    padded = (counts + MOE_BLOCK - 1) // MOE_BLOCK * MOE_BLOCK
    pend = jnp.cumsum(padded)
    pstart = (pend - padded).astype(jnp.int32)
    n_blocks = (2 * t + MOE_BLOCK - 1) // MOE_BLOCK + N_EXPERTS
    block_start = jnp.arange(n_blocks, dtype=jnp.int32) * MOE_BLOCK
    block_expert = jnp.minimum(jnp.sum((pend[None, :] <= block_start[:, None]).astype(jnp.int32), axis=1),
                               N_EXPERTS - 1).astype(jnp.int32)
    n_used = (pend[-1] // MOE_BLOCK).astype(jnp.int32).reshape(1)
    return pstart, block_expert, n_used, n_blocks * MOE_BLOCK


def _moe(h, gain, w_group, b_group, w_expert, b_expert, w_gu, w_down, final_gain, *, final, tm):
    t, _ = h.shape
    w_router = jnp.concatenate([w_group, w_expert], axis=1)
    b_router = jnp.concatenate([b_group, b_expert], axis=0)
    xn, code, gates, counts = _router(h, gain, w_router, b_router, tm=tm)
    pstart, block_expert, n_used, cap = _block_table(counts[0], t)
    code0, code1 = code[:, 0], code[:, 1]
    xs = _dispatch(xn, pstart, code0, code1, cap, tm=tm)
    ys = _experts(xs, w_gu, w_down, block_expert, n_used)
    return _combine(h, gates, pstart, code0, code1, ys, final_gain, tm=tm, final=final)


def _pick(n, pref):
    for c in pref:
        if n % c == 0:
            return c
    return n


def kernel(x, norm_mix, norm_ffn, gdn_w_in, gdn_conv_w, gdn_a_log, gdn_dt_bias, gdn_norm_w,
           gdn_w_out, sc_w_in, sc_conv_w, sc_w_out, moe_w_group, moe_b_group, moe_w_expert,
           moe_b_expert, moe_w_gu, moe_w_down, norm_final):
    batch, seq, d = x.shape
    t = batch * seq
    h = x.reshape(t, d)
    tm = _pick(t, (1024, 512, 256))
    tb = _pick(seq, (256, 128, 64))

    w_in = gdn_w_in[0]
    proj = _norm_matmul(h, norm_mix[0], w_in[:, :MAIN_DIM].astype(BF16), tm=tm, tn=1024,
                        out_dtype=F32)
    ba = _norm_side(h, norm_mix[0], w_in[:, MAIN_DIM:], tm=tm)
    o = _gdn_core(proj, ba, gdn_conv_w[0], gdn_a_log[0], gdn_dt_bias[0], gdn_norm_w[0],
                  batch=batch, seq=seq, tb=tb, hp=GDN_HEADS_PER_STEP)
    h = _matmul_res(o, gdn_w_out[0].astype(BF16), h, tm=tm, tn=1024)
    h = _moe(h, norm_ffn[0], moe_w_group[0], moe_b_group[0], moe_w_expert[0], moe_b_expert[0],
             moe_w_gu[0], moe_w_down[0], norm_final, final=False, tm=_pick(t, (256,)))

    proj = _norm_matmul(h, norm_mix[1], sc_w_in[0].astype(BF16), tm=tm, tn=1024, out_dtype=F32)
    y = _sconv(proj, sc_conv_w[0], batch=batch, seq=seq, tm=_pick(seq, (512, 256)), tn=512)
    h = _matmul_res(y, sc_w_out[0].astype(BF16), h, tm=tm, tn=1024)
    h = _moe(h, norm_ffn[1], moe_w_group[1], moe_b_group[1], moe_w_expert[1], moe_b_expert[1],
             moe_w_gu[1], moe_w_down[1], norm_final, final=True, tm=_pick(t, (256,)))
    return h.reshape(batch, seq, d)
```

```python
import functools

import jax
import jax.numpy as jnp
from jax import lax
from jax.experimental import pallas as pl
from jax.experimental.pallas import tpu as pltpu

EPS = 1e-6
F32 = jnp.float32
BF16 = jnp.bfloat16
HIGHEST = lax.Precision.HIGHEST

QK_HEADS = 16
V_HEADS = 32
HEAD_DIM = 128
KEY_DIM = QK_HEADS * HEAD_DIM
VAL_DIM = V_HEADS * HEAD_DIM
QKV_DIM = 2 * KEY_DIM + VAL_DIM
MAIN_DIM = QKV_DIM + VAL_DIM
GDN_CONV = 4
CHUNK = 64
HALO = 8
GDN_HEADS_PER_STEP = 4
GATE_ROWS = 16

SC_WIDTH = 3

N_GROUPS = 8
EXPERTS_PER_GROUP = 8
N_EXPERTS = N_GROUPS * EXPERTS_PER_GROUP
D_EXPERT = 512
MOE_BLOCK = 256
RANK_BITS = 16

VMEM_LIMIT = 56 * 1024 * 1024


def _cparams(sem):
    return pltpu.CompilerParams(dimension_semantics=sem, vmem_limit_bytes=VMEM_LIMIT)


def _rms(x, gain):
    return x * lax.rsqrt(jnp.mean(x * x, axis=-1, keepdims=True) + EPS) * gain


def _sigmoid(x):
    return 0.5 * jnp.tanh(0.5 * x) + 0.5


def _softplus(x):
    return jnp.maximum(x, 0.0) + jnp.log1p(jnp.exp(-jnp.abs(x)))


def _norm_matmul_kernel(x_ref, g_ref, w_ref, o_ref, xn_ref):
    @pl.when(pl.program_id(1) == 0)
    def _():
        xn_ref[...] = _rms(x_ref[...], g_ref[...]).astype(BF16)

    o_ref[...] = jnp.dot(xn_ref[...], w_ref[...], preferred_element_type=F32).astype(o_ref.dtype)


def _norm_matmul(x, gain, w, *, tm, tn, out_dtype):
    t, k = x.shape
    n = w.shape[1]
    return pl.pallas_call(
        _norm_matmul_kernel,
        out_shape=jax.ShapeDtypeStruct((t, n), out_dtype),
        grid=(t // tm, n // tn),
        in_specs=[
            pl.BlockSpec((tm, k), lambda i, j: (i, 0)),
            pl.BlockSpec((1, k), lambda i, j: (0, 0)),
            pl.BlockSpec((k, tn), lambda i, j: (0, j)),
        ],
        out_specs=pl.BlockSpec((tm, tn), lambda i, j: (i, j)),
        scratch_shapes=[pltpu.VMEM((tm, k), BF16)],
        compiler_params=_cparams(("parallel", "arbitrary")),
        name="norm_matmul",
    )(x, gain.reshape(1, k), w)


def _norm_matmul_side_kernel(x_ref, g_ref, w_ref, ws_ref, o_ref, os_ref, xn_ref):
    @pl.when(pl.program_id(1) == 0)
    def _():
        xn = _rms(x_ref[...], g_ref[...])
        xn_ref[...] = xn.astype(BF16)
        os_ref[...] = jnp.dot(xn, ws_ref[...], precision=HIGHEST, preferred_element_type=F32)

    o_ref[...] = jnp.dot(xn_ref[...], w_ref[...], preferred_element_type=F32).astype(o_ref.dtype)


def _norm_matmul_side(x, gain, w, w_side, *, tm, tn, out_dtype):
    t, k = x.shape
    n, ns = w.shape[1], w_side.shape[1]
    return pl.pallas_call(
        _norm_matmul_side_kernel,
        out_shape=(jax.ShapeDtypeStruct((t, n), out_dtype), jax.ShapeDtypeStruct((t, ns), F32)),
        grid=(t // tm, n // tn),
        in_specs=[
            pl.BlockSpec((tm, k), lambda i, j: (i, 0)),
            pl.BlockSpec((1, k), lambda i, j: (0, 0)),
            pl.BlockSpec((k, tn), lambda i, j: (0, j)),
            pl.BlockSpec((k, ns), lambda i, j: (0, 0)),
        ],
        out_specs=(pl.BlockSpec((tm, tn), lambda i, j: (i, j)),
                   pl.BlockSpec((tm, ns), lambda i, j: (i, 0))),
        scratch_shapes=[pltpu.VMEM((tm, k), BF16)],
        compiler_params=_cparams(("parallel", "arbitrary")),
        name="norm_matmul_side",
    )(x, gain.reshape(1, k), w, w_side)


def _matmul_res_kernel(a_ref, w_ref, r_ref, o_ref):
    o_ref[...] = r_ref[...] + jnp.dot(a_ref[...], w_ref[...], preferred_element_type=F32)


def _matmul_res(a, w, res, *, tm, tn):
    t, k = a.shape
    n = w.shape[1]
    return pl.pallas_call(
        _matmul_res_kernel,
        out_shape=jax.ShapeDtypeStruct((t, n), F32),
        grid=(t // tm, n // tn),
        in_specs=[
            pl.BlockSpec((tm, k), lambda i, j: (i, 0)),
            pl.BlockSpec((k, tn), lambda i, j: (0, j)),
            pl.BlockSpec((tm, tn), lambda i, j: (i, j)),
        ],
        out_specs=pl.BlockSpec((tm, tn), lambda i, j: (i, j)),
        compiler_params=_cparams(("parallel", "arbitrary")),
        name="matmul_res",
    )(a, w, res)


def _gdn_kernel(q_ref, k_ref, v_ref, z_ref, row_ref, cwq_ref, cwk_ref, cwv_ref,
                alog_ref, dtb_ref, nw_ref, o_ref, state_ref, qbuf, kbuf, vbuf, *, tb, hp):
    nchunk = tb // CHUNK

    @pl.when(pl.program_id(2) == 0)
    def _():
        state_ref[...] = jnp.zeros_like(state_ref)
        qbuf[0:HALO, :] = jnp.zeros((HALO, qbuf.shape[1]), F32)
        kbuf[0:HALO, :] = jnp.zeros((HALO, kbuf.shape[1]), F32)
        vbuf[0:HALO, :] = jnp.zeros((HALO, vbuf.shape[1]), F32)

    def conv_silu(x_ref, buf, cw_ref):
        buf[HALO:HALO + tb, :] = x_ref[...].astype(F32)
        xb = buf[...]
        acc = cw_ref[0:1, :] * xb
        for kk in range(1, GDN_CONV):
            acc = cw_ref[kk:kk + 1, :] * xb + pltpu.roll(acc, 1, axis=0)
        buf[0:HALO, :] = buf[tb:tb + HALO, :]
        acc = acc[HALO:, :]
        return acc * _sigmoid(acc)

    def l2norm(x):
        return x * lax.rsqrt(jnp.sum(x * x, axis=-1, keepdims=True) + EPS)

    q_all = conv_silu(q_ref, qbuf, cwq_ref)
    k_all = conv_silu(k_ref, kbuf, cwk_ref)
    v_all = conv_silu(v_ref, vbuf, cwv_ref)

    ri = lax.broadcasted_iota(jnp.int32, (tb, tb), 0)
    ci = lax.broadcasted_iota(jnp.int32, (tb, tb), 1)
    same_chunk = (ri // CHUNK) == (ci // CHUNK)
    cum_tot = jnp.concatenate([jnp.where(same_chunk & (ri <= ci), 1.0, 0.0),
                               jnp.where(same_chunk, 1.0, 0.0)], axis=1).astype(BF16)
    si = lax.broadcasted_iota(jnp.int32, (3 * GATE_ROWS, GATE_ROWS), 0)
    sj = lax.broadcasted_iota(jnp.int32, (3 * GATE_ROWS, GATE_ROWS), 1)
    fold3 = jnp.where(si % GATE_ROWS == sj, 1.0, 0.0).astype(BF16)

    def split3(x):
        hi = x.astype(BF16).astype(F32)
        r1 = x - hi
        mid = r1.astype(BF16).astype(F32)
        lo = r1 - mid
        return jnp.concatenate([hi, mid, lo], axis=0).astype(BF16)

    r64 = lax.broadcasted_iota(jnp.int32, (CHUNK, CHUNK), 0)
    c64 = lax.broadcasted_iota(jnp.int32, (CHUNK, CHUNK), 1)
    tril = c64 <= r64
    strict = c64 < r64
    eye = jnp.where(c64 == r64, 1.0, 0.0).astype(F32)

    def mm(a, b):
        return lax.dot_general(a, b.astype(BF16), (((1,), (0,)), ((), ())),
                               preferred_element_type=F32)

    def mm_nt(a, b):
        return lax.dot_general(a.astype(BF16), b.astype(BF16), (((1,), (1,)), ((), ())),
                               preferred_element_type=F32)

    def mm_tn(a, b):
        return lax.dot_general(a.astype(BF16), b.astype(BF16), (((0,), (0,)), ((), ())),
                               preferred_element_type=F32)

    chunks = range(nchunk)
    rows = [slice(c * CHUNK, (c + 1) * CHUNK) for c in chunks]
    pairs = [(p, c) for p in range(hp) for c in chunks]
    heads = [(p, c, j) for p in range(hp) for c in chunks for j in range(2)]

    qc, kc, gate_col, gc_row = {}, {}, {}, {}
    for p in range(hp):
        hl = slice(p * HEAD_DIM, (p + 1) * HEAD_DIM)
        q = l2norm(q_all[:, hl]) * (HEAD_DIM ** -0.5)
        k = l2norm(k_all[:, hl])
        for c in chunks:
            qc[p, c], kc[p, c] = q[rows[c]], k[rows[c]]
        alog, dtb = alog_ref[p].reshape(2, 1), dtb_ref[p].reshape(2, 1)
        beta_r = _sigmoid(row_ref[p, 0:2, :])
        g_r = -jnp.exp(alog) * _softplus(row_ref[p, 2:4, :] + dtb)
        r3 = jnp.dot(split3(g_r), cum_tot, preferred_element_type=F32)
        r3 = (r3[0:2, :] + r3[2:4, :]) + r3[4:6, :]
        gc_r, gtot_r = r3[:, :tb], r3[:, tb:]
        egc_r = jnp.exp(gc_r)
        gc_row[p] = gc_r
        gate_rows = jnp.concatenate(
            [beta_r, gc_r, egc_r, jnp.exp(gtot_r - gc_r), jnp.exp(gtot_r), beta_r * egc_r,
             jnp.zeros((GATE_ROWS - 12, tb), F32)], axis=0)
        gate_col[p] = lax.dot_general(split3(gate_rows), fold3, (((0,), (0,)), ((), ())),
                                      preferred_element_type=F32)

    kk_t = {pc: mm_nt(kc[pc], kc[pc]) for pc in pairs}
    qk_t = {pc: mm_nt(qc[pc], kc[pc]) for pc in pairs}

    beta, egc, kdec, etot, bege, lower, attn = {}, {}, {}, {}, {}, {}, {}
    for h in heads:
        p, c, j = h
        col = lambda i: gate_col[p][rows[c], i + j:i + j + 1]
        beta[h], gcc, egc[h], kdec[h], bege[h] = col(0), col(2), col(4), col(6), col(10)
        etot[h] = col(8)[0:1, :]
        gcr = gc_row[p][j:j + 1, rows[c]]
        decay = jnp.exp(jnp.where(tril, gcc - gcr, -jnp.inf))
        lower[h] = jnp.where(strict, beta[h] * kk_t[p, c] * decay, 0.0)
        attn[h] = jnp.where(tril, qk_t[p, c] * decay, 0.0)

    inv = {h: eye - lower[h] for h in heads}
    power = dict(lower)
    for _ in range(5):
        power = {h: mm(power[h], power[h]) for h in heads}
        inv = {h: inv[h] + mm(inv[h], power[h]) for h in heads}

    uw = {}
    for h in heads:
        p, c, j = h
        vl = slice((2 * p + j) * HEAD_DIM, (2 * p + j + 1) * HEAD_DIM)
        rhs = jnp.concatenate([v_all[rows[c], vl] * beta[h], kc[p, c] * bege[h]], axis=1)
        uw[h] = mm(inv[h], rhs)
    nk = {h: mm_tn(kc[h[0], h[1]] * kdec[h], uw[h]) for h in heads}
    ao = {h: mm(attn[h], uw[h]) for h in heads}
    qp = {h: (qc[h[0], h[1]] * egc[h] - ao[h][:, HEAD_DIM:]).astype(BF16) for h in heads}

    state = {(p, j): state_ref[2 * p + j] for p in range(hp) for j in range(2)}
    seen = {}
    for c in chunks:
        for p in range(hp):
            for j in range(2):
                h = (p, c, j)
                s_bf = state[p, j].astype(BF16)
                seen[h] = s_bf
                n_c, k_c = nk[h][:, :HEAD_DIM], nk[h][:, HEAD_DIM:]
                state[p, j] = state[p, j] * etot[h] + (n_c - mm(k_c, s_bf))
    for p in range(hp):
        for j in range(2):
            state_ref[2 * p + j] = state[p, j]

    nw = nw_ref[...]
    for h in heads:
        p, c, j = h
        vl = slice((2 * p + j) * HEAD_DIM, (2 * p + j + 1) * HEAD_DIM)
        o = jnp.dot(qp[h], seen[h], preferred_element_type=F32) + ao[h][:, :HEAD_DIM]
        zc = z_ref[rows[c], vl].astype(F32)
        o = o * lax.rsqrt(jnp.mean(o * o, axis=-1, keepdims=True) + EPS) * nw
        o_ref[rows[c], vl] = (o * (zc * _sigmoid(zc))).astype(o_ref.dtype)


def _gdn_core(proj, ba, conv_w, a_log, dt_bias, norm_w, *, batch, seq, tb, hp):
    proj3 = proj.reshape(batch, seq, MAIN_DIM)
    ba4 = ba.reshape(batch, seq, 2, QK_HEADS, 2)
    rows = jnp.transpose(ba4, (0, 3, 2, 4, 1)).reshape(batch, QK_HEADS, 4, seq)
    alog2 = a_log.reshape(QK_HEADS, 1, 2)
    dtb2 = dt_bias.reshape(QK_HEADS, 1, 2)
    qw, vw = hp * HEAD_DIM, 2 * hp * HEAD_DIM
    kq = KEY_DIM // qw
    vq = (2 * KEY_DIM) // vw
    zq = QKV_DIM // vw
    out = pl.pallas_call(
        functools.partial(_gdn_kernel, tb=tb, hp=hp),
        out_shape=jax.ShapeDtypeStruct((batch, seq, VAL_DIM), BF16),
        grid=(batch, QK_HEADS // hp, seq // tb),
        in_specs=[
            pl.BlockSpec((None, tb, qw), lambda b, h, t: (b, t, h)),
            pl.BlockSpec((None, tb, qw), lambda b, h, t: (b, t, kq + h)),
            pl.BlockSpec((None, tb, vw), lambda b, h, t: (b, t, vq + h)),
            pl.BlockSpec((None, tb, vw), lambda b, h, t: (b, t, zq + h)),
            pl.BlockSpec((None, hp, 4, tb), lambda b, h, t: (b, h, 0, t)),
            pl.BlockSpec((GDN_CONV, qw), lambda b, h, t: (0, h)),
            pl.BlockSpec((GDN_CONV, qw), lambda b, h, t: (0, kq + h)),
            pl.BlockSpec((GDN_CONV, vw), lambda b, h, t: (0, vq + h)),
            pl.BlockSpec((hp, 1, 2), lambda b, h, t: (h, 0, 0)),
            pl.BlockSpec((hp, 1, 2), lambda b, h, t: (h, 0, 0)),
            pl.BlockSpec((1, HEAD_DIM), lambda b, h, t: (0, 0)),
        ],
        out_specs=pl.BlockSpec((None, tb, vw), lambda b, h, t: (b, t, h)),
        scratch_shapes=[
            pltpu.VMEM((2 * hp, HEAD_DIM, HEAD_DIM), F32),
            pltpu.VMEM((tb + HALO, qw), F32),
            pltpu.VMEM((tb + HALO, qw), F32),
            pltpu.VMEM((tb + HALO, vw), F32),
        ],
        compiler_params=_cparams(("parallel", "parallel", "arbitrary")),
        name="gdn_core",
    )(proj3, proj3, proj3, proj3, rows, conv_w, conv_w, conv_w, alog2, dtb2,
      norm_w.reshape(1, HEAD_DIM))
    return out.reshape(batch * seq, VAL_DIM)


def _sconv_kernel(b_ref, c_ref, h_ref, cw_ref, o_ref, buf, *, tm):
    @pl.when(pl.program_id(2) == 0)
    def _():
        buf[0:HALO, :] = jnp.zeros((HALO, buf.shape[1]), F32)

    buf[HALO:HALO + tm, :] = c_ref[...].astype(F32) * h_ref[...].astype(F32)
    off = HALO - (SC_WIDTH - 1)
    acc = cw_ref[0:1, :] * buf[off:off + tm, :]
    for kk in range(1, SC_WIDTH):
        acc = acc + cw_ref[kk:kk + 1, :] * buf[off + kk:off + kk + tm, :]
    buf[0:HALO, :] = buf[tm:tm + HALO, :]
    o_ref[...] = (b_ref[...].astype(F32) * acc).astype(o_ref.dtype)


def _sconv(proj, conv_w, *, batch, seq, tm, tn):
    d = conv_w.shape[1]
    proj3 = proj.reshape(batch, seq, 3 * d)
    nj = d // tn
    out = pl.pallas_call(
        functools.partial(_sconv_kernel, tm=tm),
        out_shape=jax.ShapeDtypeStruct((batch, seq, d), BF16),
        grid=(batch, nj, seq // tm),
        in_specs=[
            pl.BlockSpec((None, tm, tn), lambda b, j, t: (b, t, j)),
            pl.BlockSpec((None, tm, tn), lambda b, j, t: (b, t, nj + j)),
            pl.BlockSpec((None, tm, tn), lambda b, j, t: (b, t, 2 * nj + j)),
            pl.BlockSpec((SC_WIDTH, tn), lambda b, j, t: (0, j)),
        ],
        out_specs=pl.BlockSpec((None, tm, tn), lambda b, j, t: (b, t, j)),
        scratch_shapes=[pltpu.VMEM((tm + HALO, tn), F32)],
        compiler_params=_cparams(("parallel", "parallel", "arbitrary")),
        name="sconv",
    )(proj3, proj3, proj3, conv_w)
    return out.reshape(batch * seq, d)


def _router_kernel(h_ref, g_ref, wr_ref, br_ref, xn_ref, code_ref, gate_ref, cnt_ref, carry_ref):
    @pl.when(pl.program_id(0) == 0)
    def _():
        carry_ref[...] = jnp.zeros_like(carry_ref)

    xn = _rms(h_ref[...], g_ref[...])
    xn_ref[...] = xn
    logits = jnp.dot(xn, wr_ref[...], precision=HIGHEST, preferred_element_type=F32) + br_ref[...]
    tm = logits.shape[0]
    glog = logits[:, 0:N_GROUPS]
    elog = logits[:, N_GROUPS:N_GROUPS + N_EXPERTS]
    gl = lax.broadcasted_iota(jnp.int32, (tm, N_GROUPS), 1)
    gmax = jnp.max(glog, axis=-1, keepdims=True)
    group = jnp.min(jnp.where(glog == gmax, gl, N_GROUPS), axis=-1, keepdims=True)
    p_group = 1.0 / jnp.sum(jnp.exp(glog - gmax), axis=-1, keepdims=True)
    el = lax.broadcasted_iota(jnp.int32, (tm, N_EXPERTS), 1)
    neg = jnp.float32(-jnp.inf)
    within = jnp.where((el // EXPERTS_PER_GROUP) == group, elog, neg)
    m1 = jnp.max(within, axis=-1, keepdims=True)
    i1 = jnp.min(jnp.where(within == m1, el, N_EXPERTS), axis=-1, keepdims=True)
    rest = jnp.where(el == i1, neg, within)
    m2 = jnp.max(rest, axis=-1, keepdims=True)
    i2 = jnp.min(jnp.where(rest == m2, el, N_EXPERTS), axis=-1, keepdims=True)
    e2 = jnp.exp(m2 - m1)
    g1 = p_group / (1.0 + e2)
    g2 = p_group * e2 / (1.0 + e2)

    pick1, pick2 = el == i1, el == i2
    onehot = jnp.where(pick1 | pick2, 1.0, 0.0)
    rr = lax.broadcasted_iota(jnp.int32, (tm, tm), 0)
    cc = lax.broadcasted_iota(jnp.int32, (tm, tm), 1)
    before = jnp.where(cc < rr, 1.0, 0.0).astype(BF16)
    prefix = jnp.dot(before, onehot.astype(BF16), preferred_element_type=F32) + carry_ref[...]
    r1 = jnp.sum(jnp.where(pick1, prefix, 0.0), axis=-1, keepdims=True).astype(jnp.int32)
    r2 = jnp.sum(jnp.where(pick2, prefix, 0.0), axis=-1, keepdims=True).astype(jnp.int32)
    carry_ref[...] = carry_ref[...] + jnp.sum(onehot, axis=0, keepdims=True)
    cnt_ref[...] = carry_ref[...]

    lane2 = lax.broadcasted_iota(jnp.int32, (tm, 2), 1)
    code1 = jnp.left_shift(i1, RANK_BITS) | r1
    code2 = jnp.left_shift(i2, RANK_BITS) | r2
    code_ref[...] = jnp.where(lane2 == 0, code1, code2)
    gate_ref[...] = jnp.where(lane2 == 0, g1, g2)


def _router(h, gain, w_router, b_router, *, tm):
    t, d = h.shape
    nr = w_router.shape[1]
    return pl.pallas_call(
        _router_kernel,
        out_shape=(jax.ShapeDtypeStruct((t, d), F32),
                   jax.ShapeDtypeStruct((t, 2), jnp.int32),
                   jax.ShapeDtypeStruct((t, 2), F32),
                   jax.ShapeDtypeStruct((1, N_EXPERTS), F32)),
        grid=(t // tm,),
        in_specs=[
            pl.BlockSpec((tm, d), lambda i: (i, 0)),
            pl.BlockSpec((1, d), lambda i: (0, 0)),
            pl.BlockSpec((d, nr), lambda i: (0, 0)),
            pl.BlockSpec((1, nr), lambda i: (0, 0)),
        ],
        out_specs=(pl.BlockSpec((tm, d), lambda i: (i, 0)),
                   pl.BlockSpec((tm, 2), lambda i: (i, 0)),
                   pl.BlockSpec((tm, 2), lambda i: (i, 0)),
                   pl.BlockSpec((1, N_EXPERTS), lambda i: (0, 0))),
        scratch_shapes=[pltpu.VMEM((1, N_EXPERTS), F32)],
        compiler_params=_cparams(("arbitrary",)),
        name="moe_router",
    )(h, gain.reshape(1, d), w_router, b_router.reshape(1, nr))


def _slot(pstart_ref, code):
    return pstart_ref[lax.shift_right_logical(code, RANK_BITS)] + (code & ((1 << RANK_BITS) - 1))


def _dispatch_kernel(ps_ref, pf_ref, c0_ref, c1_ref, x_ref, xs_ref, zbuf, sem, zsem, *, tm, nb):
    base = pl.program_id(0) * tm

    @pl.when(pl.program_id(0) == 0)
    def _():
        zbuf[...] = jnp.zeros_like(zbuf)

        def zero_copy(b):
            return pltpu.make_async_copy(zbuf, xs_ref.at[pl.ds(b * MOE_BLOCK, MOE_BLOCK), :], zsem)

        def zstart(b, carry):
            @pl.when(pf_ref[b] != 0)
            def _():
                zero_copy(b).start()
            return carry

        def zwait(b, carry):
            @pl.when(pf_ref[b] != 0)
            def _():
                zero_copy(b).wait()
            return carry

        lax.fori_loop(0, nb, zstart, 0)
        lax.fori_loop(0, nb, zwait, 0)

    def row_copy(r, dest):
        return pltpu.make_async_copy(x_ref.at[pl.ds(r, 1), :], xs_ref.at[pl.ds(dest, 1), :], sem)

    def issue(r, carry):
        row_copy(r, _slot(ps_ref, c0_ref[base + r])).start()
        row_copy(r, _slot(ps_ref, c1_ref[base + r])).start()
        return carry

    lax.fori_loop(0, tm, issue, 0, unroll=8)
    whole = pltpu.make_async_copy(x_ref, xs_ref.at[pl.ds(0, tm), :], sem)
    whole.wait()
    whole.wait()


def _dispatch(xn, pstart, partial_block, code0, code1, cap, *, tm):
    t, d = xn.shape
    return pl.pallas_call(
        functools.partial(_dispatch_kernel, tm=tm, nb=cap // MOE_BLOCK),
        out_shape=jax.ShapeDtypeStruct((cap, d), F32),
        grid_spec=pltpu.PrefetchScalarGridSpec(
            num_scalar_prefetch=4,
            grid=(t // tm,),
            in_specs=[pl.BlockSpec((tm, d), lambda i, ps, pf, c0, c1: (i, 0))],
            out_specs=pl.BlockSpec(memory_space=pl.ANY),
            scratch_shapes=[pltpu.VMEM((MOE_BLOCK, d), F32), pltpu.SemaphoreType.DMA,
                            pltpu.SemaphoreType.DMA],
        ),
        compiler_params=_cparams(("arbitrary",)),
        name="moe_dispatch",
    )(pstart, partial_block, code0, code1, xn)


def _expert_kernel(be_ref, nu_ref, x_ref, wgu_ref, wd_ref, o_ref, wgu_bf, wd_bf):
    i = pl.program_id(0)
    used = i < nu_ref[0]
    new_expert = (i == 0) | (be_ref[i] != be_ref[jnp.maximum(i - 1, 0)])

    @pl.when(used & new_expert)
    def _():
        wgu_bf[...] = wgu_ref[...].astype(BF16)
        wd_bf[...] = wd_ref[...].astype(BF16)

    @pl.when(used)
    def _():
        x = x_ref[...].astype(BF16)
        gu = jnp.dot(x, wgu_bf[...], preferred_element_type=F32)
        gt, up = gu[:, :D_EXPERT], gu[:, D_EXPERT:]
        act = (gt * _sigmoid(gt) * up).astype(BF16)
        o_ref[...] = jnp.dot(act, wd_bf[...], preferred_element_type=F32)

    @pl.when(jnp.logical_not(used))
    def _():
        o_ref[...] = jnp.zeros_like(o_ref)


def _experts(xs, w_gu, w_down, layer, block_expert, n_used):
    cap, d = xs.shape
    nb = cap // MOE_BLOCK

    def blk(i, be, nu):
        return jnp.maximum(jnp.minimum(i, nu[0] - 1), 0)

    return pl.pallas_call(
        _expert_kernel,
        out_shape=jax.ShapeDtypeStruct((cap, d), F32),
        grid_spec=pltpu.PrefetchScalarGridSpec(
            num_scalar_prefetch=2,
            grid=(nb,),
            in_specs=[
                pl.BlockSpec((MOE_BLOCK, d), lambda i, be, nu: (blk(i, be, nu), 0)),
                pl.BlockSpec((None, None, d, 2 * D_EXPERT),
                             lambda i, be, nu: (layer, be[blk(i, be, nu)], 0, 0)),
                pl.BlockSpec((None, None, D_EXPERT, d),
                             lambda i, be, nu: (layer, be[blk(i, be, nu)], 0, 0)),
            ],
            out_specs=pl.BlockSpec((MOE_BLOCK, d), lambda i, be, nu: (i, 0)),
            scratch_shapes=[pltpu.VMEM((d, 2 * D_EXPERT), BF16), pltpu.VMEM((D_EXPERT, d), BF16)],
        ),
        compiler_params=_cparams(("arbitrary",)),
        name="moe_experts",
    )(block_expert, n_used, xs, w_gu, w_down)


def _combine_kernel(ps_ref, c0_ref, c1_ref, h_ref, gate_ref, fg_ref, ys_ref, o_ref, buf, sem, *,
                    tm, final):
    base = pl.program_id(0) * tm

    def row_copy(r, k, src):
        return pltpu.make_async_copy(ys_ref.at[pl.ds(src, 1), :], buf.at[k, pl.ds(r, 1), :], sem)

    def issue(r, carry):
        row_copy(r, 0, _slot(ps_ref, c0_ref[base + r])).start()
        row_copy(r, 1, _slot(ps_ref, c1_ref[base + r])).start()
        return carry

    lax.fori_loop(0, tm, issue, 0, unroll=8)
    for k in range(2):
        pltpu.make_async_copy(ys_ref.at[pl.ds(0, tm), :], buf.at[k], sem).wait()
    gate = gate_ref[...]
    y = h_ref[...] + (gate[:, 0:1] * buf[0] + gate[:, 1:2] * buf[1])
    if final:
        y = _rms(y, fg_ref[...])
    o_ref[...] = y


def _combine(h, gates, pstart, code0, code1, ys, final_gain, *, tm, final):
    t, d = h.shape
    return pl.pallas_call(
        functools.partial(_combine_kernel, tm=tm, final=final),
        out_shape=jax.ShapeDtypeStruct((t, d), F32),
        grid_spec=pltpu.PrefetchScalarGridSpec(
            num_scalar_prefetch=3,
            grid=(t // tm,),
            in_specs=[
                pl.BlockSpec((tm, d), lambda i, ps, c0, c1: (i, 0)),
                pl.BlockSpec((tm, 2), lambda i, ps, c0, c1: (i, 0)),
                pl.BlockSpec((1, d), lambda i, ps, c0, c1: (0, 0)),
                pl.BlockSpec(memory_space=pl.ANY),
            ],
            out_specs=pl.BlockSpec((tm, d), lambda i, ps, c0, c1: (i, 0)),
            scratch_shapes=[pltpu.VMEM((2, tm, d), F32), pltpu.SemaphoreType.DMA],
        ),
        compiler_params=_cparams(("arbitrary",)),
        name="moe_combine",
    )(pstart, code0, code1, h, gates, final_gain.reshape(1, d), ys)


def _block_table(counts, t):
    counts = counts.astype(jnp.int32)
    padded = (counts + MOE_BLOCK - 1) // MOE_BLOCK * MOE_BLOCK
    pend = jnp.cumsum(padded)
    pstart = (pend - padded).astype(jnp.int32)
    n_blocks = (2 * t + MOE_BLOCK - 1) // MOE_BLOCK + N_EXPERTS
    block_start = jnp.arange(n_blocks, dtype=jnp.int32) * MOE_BLOCK
    block_expert = jnp.minimum(jnp.sum((pend[None, :] <= block_start[:, None]).astype(jnp.int32), axis=1),
                               N_EXPERTS - 1).astype(jnp.int32)
    n_used = (pend[-1] // MOE_BLOCK).astype(jnp.int32).reshape(1)
    filled_to = (pstart + counts)[block_expert]
    partial_block = ((block_start + MOE_BLOCK > filled_to) | (block_start >= pend[-1])).astype(jnp.int32)
    return pstart, block_expert, n_used, partial_block, n_blocks * MOE_BLOCK


def _moe(h, gain, w_group, b_group, w_expert, b_expert, w_gu, w_down, layer, final_gain, *,
         final, tm):
    t, _ = h.shape
    w_router = jnp.concatenate([w_group, w_expert], axis=1)
    b_router = jnp.concatenate([b_group, b_expert], axis=0)
    xn, code, gates, counts = _router(h, gain, w_router, b_router, tm=tm)
    pstart, block_expert, n_used, partial_block, cap = _block_table(counts[0], t)
    code0, code1 = code[:, 0], code[:, 1]
    xs = _dispatch(xn, pstart, partial_block, code0, code1, cap, tm=tm)
    ys = _experts(xs, w_gu, w_down, layer, block_expert, n_used)
    return _combine(h, gates, pstart, code0, code1, ys, final_gain, tm=tm, final=final)


def _pick(n, pref):
    for c in pref:
        if n % c == 0:
            return c
    return n


def kernel(x, norm_mix, norm_ffn, gdn_w_in, gdn_conv_w, gdn_a_log, gdn_dt_bias, gdn_norm_w,
           gdn_w_out, sc_w_in, sc_conv_w, sc_w_out, moe_w_group, moe_b_group, moe_w_expert,
           moe_b_expert, moe_w_gu, moe_w_down, norm_final):
    batch, seq, d = x.shape
    t = batch * seq
    h = x.reshape(t, d)
    tm = _pick(t, (1024, 512, 256))
    tb = _pick(seq, (256, 128, 64))

    w_in = gdn_w_in[0]
    proj, ba = _norm_matmul_side(h, norm_mix[0], w_in[:, :MAIN_DIM].astype(BF16),
                                 w_in[:, MAIN_DIM:], tm=tm, tn=1024, out_dtype=F32)
    o = _gdn_core(proj, ba, gdn_conv_w[0], gdn_a_log[0], gdn_dt_bias[0], gdn_norm_w[0],
                  batch=batch, seq=seq, tb=tb, hp=GDN_HEADS_PER_STEP)
    h = _matmul_res(o, gdn_w_out[0].astype(BF16), h, tm=tm, tn=1024)
    h = _moe(h, norm_ffn[0], moe_w_group[0], moe_b_group[0], moe_w_expert[0], moe_b_expert[0],
             moe_w_gu, moe_w_down, 0, norm_final, final=False, tm=_pick(t, (256,)))

    proj = _norm_matmul(h, norm_mix[1], sc_w_in[0].astype(BF16), tm=tm, tn=1024, out_dtype=F32)
    y = _sconv(proj, sc_conv_w[0], batch=batch, seq=seq, tm=_pick(seq, (512, 256)), tn=512)
    h = _matmul_res(y, sc_w_out[0].astype(BF16), h, tm=tm, tn=1024)
    h = _moe(h, norm_ffn[1], moe_w_group[1], moe_b_group[1], moe_w_expert[1], moe_b_expert[1],
             moe_w_gu, moe_w_down, 1, norm_final, final=True, tm=_pick(t, (256,)))
    return h.reshape(batch, seq, d)
```

```python
import functools

import jax
import jax.numpy as jnp
from jax import lax
from jax.experimental import pallas as pl
from jax.experimental.pallas import tpu as pltpu

EPS = 1e-6
F32 = jnp.float32
BF16 = jnp.bfloat16

QK_HEADS = 16
V_HEADS = 32
HEAD_DIM = 128
KEY_DIM = QK_HEADS * HEAD_DIM
VAL_DIM = V_HEADS * HEAD_DIM
QKV_DIM = 2 * KEY_DIM + VAL_DIM
MAIN_DIM = QKV_DIM + VAL_DIM
GDN_CONV = 4
CHUNK = 64
HALO = 8
GDN_HEADS_PER_STEP = 4
GATE_ROWS = 16

SC_WIDTH = 3

N_GROUPS = 8
EXPERTS_PER_GROUP = 8
N_EXPERTS = N_GROUPS * EXPERTS_PER_GROUP
D_EXPERT = 512
MOE_BLOCK = 512
ROUTER_WIDTH = 128
RANK_BITS = 16

VMEM_LIMIT = 56 * 1024 * 1024


def _cparams(sem):
    return pltpu.CompilerParams(dimension_semantics=sem, vmem_limit_bytes=VMEM_LIMIT)


def _rms(x, gain):
    return x * lax.rsqrt(jnp.mean(x * x, axis=-1, keepdims=True) + EPS) * gain


def _sigmoid(x):
    return 0.5 * jnp.tanh(0.5 * x) + 0.5


def _softplus(x):
    return jnp.maximum(x, 0.0) + jnp.log1p(jnp.exp(-jnp.abs(x)))


def _split2(w):
    hi = w.astype(BF16)
    lo = (w - hi.astype(F32)).astype(BF16)
    return jnp.concatenate([hi, lo], axis=-1)


def _dot_split(x, w2):
    n = w2.shape[-1] // 2
    x_hi = x.astype(BF16)
    x_lo = (x - x_hi.astype(F32)).astype(BF16)
    a = jnp.dot(x_hi, w2, preferred_element_type=F32)
    b = jnp.dot(x_lo, w2[:, :n], preferred_element_type=F32)
    return (a[:, :n] + a[:, n:]) + b


def _norm_matmul_kernel(x_ref, g_ref, w_ref, o_ref, xn_ref):
    @pl.when(pl.program_id(1) == 0)
    def _():
        xn_ref[...] = _rms(x_ref[...], g_ref[...]).astype(BF16)

    o_ref[...] = jnp.dot(xn_ref[...], w_ref[...], preferred_element_type=F32).astype(o_ref.dtype)


def _norm_matmul(x, gain, w, *, tm, tn, out_dtype):
    t, k = x.shape
    n = w.shape[1]
    return pl.pallas_call(
        _norm_matmul_kernel,
        out_shape=jax.ShapeDtypeStruct((t, n), out_dtype),
        grid=(t // tm, n // tn),
        in_specs=[
            pl.BlockSpec((tm, k), lambda i, j: (i, 0)),
            pl.BlockSpec((1, k), lambda i, j: (0, 0)),
            pl.BlockSpec((k, tn), lambda i, j: (0, j)),
        ],
        out_specs=pl.BlockSpec((tm, tn), lambda i, j: (i, j)),
        scratch_shapes=[pltpu.VMEM((tm, k), BF16)],
        compiler_params=_cparams(("parallel", "arbitrary")),
        name="norm_matmul",
    )(x, gain.reshape(1, k), w)


def _norm_matmul_side_kernel(x_ref, g_ref, w_ref, ws_ref, o_ref, os_ref, xn_ref):
    @pl.when(pl.program_id(1) == 0)
    def _():
        xn = _rms(x_ref[...], g_ref[...])
        xn_ref[...] = xn.astype(BF16)
        os_ref[...] = _dot_split(xn, ws_ref[...])

    o_ref[...] = jnp.dot(xn_ref[...], w_ref[...], preferred_element_type=F32).astype(o_ref.dtype)


def _norm_matmul_side(x, gain, w, w_side, *, tm, tn, out_dtype):
    t, k = x.shape
    n, ns = w.shape[1], w_side.shape[1]
    return pl.pallas_call(
        _norm_matmul_side_kernel,
        out_shape=(jax.ShapeDtypeStruct((t, n), out_dtype), jax.ShapeDtypeStruct((t, ns), F32)),
        grid=(t // tm, n // tn),
        in_specs=[
            pl.BlockSpec((tm, k), lambda i, j: (i, 0)),
            pl.BlockSpec((1, k), lambda i, j: (0, 0)),
            pl.BlockSpec((k, tn), lambda i, j: (0, j)),
            pl.BlockSpec((k, 2 * ns), lambda i, j: (0, 0)),
        ],
        out_specs=(pl.BlockSpec((tm, tn), lambda i, j: (i, j)),
                   pl.BlockSpec((tm, ns), lambda i, j: (i, 0))),
        scratch_shapes=[pltpu.VMEM((tm, k), BF16)],
        compiler_params=_cparams(("parallel", "arbitrary")),
        name="norm_matmul_side",
    )(x, gain.reshape(1, k), w, _split2(w_side))


def _matmul_res_kernel(a_ref, w_ref, r_ref, o_ref):
    o_ref[...] = r_ref[...] + jnp.dot(a_ref[...], w_ref[...], preferred_element_type=F32)


def _matmul_res(a, w, res, *, tm, tn):
    t, k = a.shape
    n = w.shape[1]
    return pl.pallas_call(
        _matmul_res_kernel,
        out_shape=jax.ShapeDtypeStruct((t, n), F32),
        grid=(t // tm, n // tn),
        in_specs=[
            pl.BlockSpec((tm, k), lambda i, j: (i, 0)),
            pl.BlockSpec((k, tn), lambda i, j: (0, j)),
            pl.BlockSpec((tm, tn), lambda i, j: (i, j)),
        ],
        out_specs=pl.BlockSpec((tm, tn), lambda i, j: (i, j)),
        compiler_params=_cparams(("parallel", "arbitrary")),
        name="matmul_res",
    )(a, w, res)


def _gdn_kernel(q_ref, k_ref, v_ref, z_ref, row_ref, cwq_ref, cwk_ref, cwv_ref,
                alog_ref, dtb_ref, nw_ref, o_ref, state_ref, qbuf, kbuf, vbuf, *, tb, hp):
    nchunk = tb // CHUNK

    @pl.when(pl.program_id(2) == 0)
    def _():
        state_ref[...] = jnp.zeros_like(state_ref)
        qbuf[0:HALO, :] = jnp.zeros((HALO, qbuf.shape[1]), F32)
        kbuf[0:HALO, :] = jnp.zeros((HALO, kbuf.shape[1]), F32)
        vbuf[0:HALO, :] = jnp.zeros((HALO, vbuf.shape[1]), F32)

    def conv_silu(x_ref, buf, cw_ref):
        buf[HALO:HALO + tb, :] = x_ref[...].astype(F32)
        xb = buf[...]
        acc = cw_ref[0:1, :] * xb
        for kk in range(1, GDN_CONV):
            acc = cw_ref[kk:kk + 1, :] * xb + pltpu.roll(acc, 1, axis=0)
        buf[0:HALO, :] = buf[tb:tb + HALO, :]
        acc = acc[HALO:, :]
        return acc * _sigmoid(acc)

    def l2norm(x):
        return x * lax.rsqrt(jnp.sum(x * x, axis=-1, keepdims=True) + EPS)

    q_all = conv_silu(q_ref, qbuf, cwq_ref)
    k_all = conv_silu(k_ref, kbuf, cwk_ref)
    v_all = conv_silu(v_ref, vbuf, cwv_ref)

    ri = lax.broadcasted_iota(jnp.int32, (tb, tb), 0)
    ci = lax.broadcasted_iota(jnp.int32, (tb, tb), 1)
    same_chunk = (ri // CHUNK) == (ci // CHUNK)
    cum_tot = jnp.concatenate([jnp.where(same_chunk & (ri <= ci), 1.0, 0.0),
                               jnp.where(same_chunk, 1.0, 0.0)], axis=1).astype(BF16)
    si = lax.broadcasted_iota(jnp.int32, (3 * GATE_ROWS, GATE_ROWS), 0)
    sj = lax.broadcasted_iota(jnp.int32, (3 * GATE_ROWS, GATE_ROWS), 1)
    fold3 = jnp.where(si % GATE_ROWS == sj, 1.0, 0.0).astype(BF16)

    def split3(x):
        hi = x.astype(BF16).astype(F32)
        r1 = x - hi
        mid = r1.astype(BF16).astype(F32)
        lo = r1 - mid
        return jnp.concatenate([hi, mid, lo], axis=0).astype(BF16)

    r64 = lax.broadcasted_iota(jnp.int32, (CHUNK, CHUNK), 0)
    c64 = lax.broadcasted_iota(jnp.int32, (CHUNK, CHUNK), 1)
    tril = c64 <= r64
    strict = c64 < r64
    eye = jnp.where(c64 == r64, 1.0, 0.0).astype(F32)

    def mm(a, b):
        return lax.dot_general(a, b.astype(BF16), (((1,), (0,)), ((), ())),
                               preferred_element_type=F32)

    def mm_nt(a, b):
        return lax.dot_general(a.astype(BF16), b.astype(BF16), (((1,), (1,)), ((), ())),
                               preferred_element_type=F32)

    def mm_tn(a, b):
        return lax.dot_general(a.astype(BF16), b.astype(BF16), (((0,), (0,)), ((), ())),
                               preferred_element_type=F32)

    chunks = range(nchunk)
    rows = [slice(c * CHUNK, (c + 1) * CHUNK) for c in chunks]
    pairs = [(p, c) for p in range(hp) for c in chunks]
    heads = [(p, c, j) for p in range(hp) for c in chunks for j in range(2)]

    qc, kc, gate_col, gc_row = {}, {}, {}, {}
    for p in range(hp):
        hl = slice(p * HEAD_DIM, (p + 1) * HEAD_DIM)
        q = l2norm(q_all[:, hl]) * (HEAD_DIM ** -0.5)
        k = l2norm(k_all[:, hl])
        for c in chunks:
            qc[p, c], kc[p, c] = q[rows[c]], k[rows[c]]
        alog, dtb = alog_ref[p].reshape(2, 1), dtb_ref[p].reshape(2, 1)
        beta_r = _sigmoid(row_ref[p, 0:2, :])
        g_r = -jnp.exp(alog) * _softplus(row_ref[p, 2:4, :] + dtb)
        r3 = jnp.dot(split3(g_r), cum_tot, preferred_element_type=F32)
        r3 = (r3[0:2, :] + r3[2:4, :]) + r3[4:6, :]
        gc_r, gtot_r = r3[:, :tb], r3[:, tb:]
        egc_r = jnp.exp(gc_r)
        gc_row[p] = gc_r
        gate_rows = jnp.concatenate(
            [beta_r, gc_r, egc_r, jnp.exp(gtot_r - gc_r), jnp.exp(gtot_r), beta_r * egc_r,
             jnp.zeros((GATE_ROWS - 12, tb), F32)], axis=0)
        gate_col[p] = lax.dot_general(split3(gate_rows), fold3, (((0,), (0,)), ((), ())),
                                      preferred_element_type=F32)

    kk_t = {pc: mm_nt(kc[pc], kc[pc]) for pc in pairs}
    qk_t = {pc: mm_nt(qc[pc], kc[pc]) for pc in pairs}

    beta, egc, kdec, etot, bege, lower, attn = {}, {}, {}, {}, {}, {}, {}
    for h in heads:
        p, c, j = h
        col = lambda i: gate_col[p][rows[c], i + j:i + j + 1]
        beta[h], gcc, egc[h], kdec[h], bege[h] = col(0), col(2), col(4), col(6), col(10)
        etot[h] = col(8)[0:1, :]
        gcr = gc_row[p][j:j + 1, rows[c]]
        decay = jnp.exp(jnp.where(tril, gcc - gcr, -jnp.inf))
        lower[h] = jnp.where(strict, beta[h] * kk_t[p, c] * decay, 0.0)
        attn[h] = jnp.where(tril, qk_t[p, c] * decay, 0.0)

    inv = {h: eye - lower[h] for h in heads}
    power = dict(lower)
    for _ in range(5):
        power = {h: mm(power[h], power[h]) for h in heads}
        inv = {h: inv[h] + mm(inv[h], power[h]) for h in heads}

    uw = {}
    for h in heads:
        p, c, j = h
        vl = slice((2 * p + j) * HEAD_DIM, (2 * p + j + 1) * HEAD_DIM)
        rhs = jnp.concatenate([v_all[rows[c], vl] * beta[h], kc[p, c] * bege[h]], axis=1)
        uw[h] = mm(inv[h], rhs)
    nk = {h: mm_tn(kc[h[0], h[1]] * kdec[h], uw[h]) for h in heads}
    ao = {h: mm(attn[h], uw[h]) for h in heads}
    qp = {h: (qc[h[0], h[1]] * egc[h] - ao[h][:, HEAD_DIM:]).astype(BF16) for h in heads}

    state = {(p, j): state_ref[2 * p + j] for p in range(hp) for j in range(2)}
    seen = {}
    for c in chunks:
        for p in range(hp):
            for j in range(2):
                h = (p, c, j)
                s_bf = state[p, j].astype(BF16)
                seen[h] = s_bf
                n_c, k_c = nk[h][:, :HEAD_DIM], nk[h][:, HEAD_DIM:]
                state[p, j] = state[p, j] * etot[h] + (n_c - mm(k_c, s_bf))
    for p in range(hp):
        for j in range(2):
            state_ref[2 * p + j] = state[p, j]

    nw = nw_ref[...]
    for h in heads:
        p, c, j = h
        vl = slice((2 * p + j) * HEAD_DIM, (2 * p + j + 1) * HEAD_DIM)
        o = jnp.dot(qp[h], seen[h], preferred_element_type=F32) + ao[h][:, :HEAD_DIM]
        zc = z_ref[rows[c], vl].astype(F32)
        o = o * lax.rsqrt(jnp.mean(o * o, axis=-1, keepdims=True) + EPS) * nw
        o_ref[rows[c], vl] = (o * (zc * _sigmoid(zc))).astype(o_ref.dtype)


def _gdn_core(proj, ba, conv_w, a_log, dt_bias, norm_w, *, batch, seq, tb, hp):
    proj3 = proj.reshape(batch, seq, MAIN_DIM)
    ba4 = ba.reshape(batch, seq, 2, QK_HEADS, 2)
    rows = jnp.transpose(ba4, (0, 3, 2, 4, 1)).reshape(batch, QK_HEADS, 4, seq)
    alog2 = a_log.reshape(QK_HEADS, 1, 2)
    dtb2 = dt_bias.reshape(QK_HEADS, 1, 2)
    qw, vw = hp * HEAD_DIM, 2 * hp * HEAD_DIM
    kq = KEY_DIM // qw
    vq = (2 * KEY_DIM) // vw
    zq = QKV_DIM // vw
    out = pl.pallas_call(
        functools.partial(_gdn_kernel, tb=tb, hp=hp),
        out_shape=jax.ShapeDtypeStruct((batch, seq, VAL_DIM), BF16),
        grid=(batch, QK_HEADS // hp, seq // tb),
        in_specs=[
            pl.BlockSpec((None, tb, qw), lambda b, h, t: (b, t, h)),
            pl.BlockSpec((None, tb, qw), lambda b, h, t: (b, t, kq + h)),
            pl.BlockSpec((None, tb, vw), lambda b, h, t: (b, t, vq + h)),
            pl.BlockSpec((None, tb, vw), lambda b, h, t: (b, t, zq + h)),
            pl.BlockSpec((None, hp, 4, tb), lambda b, h, t: (b, h, 0, t)),
            pl.BlockSpec((GDN_CONV, qw), lambda b, h, t: (0, h)),
            pl.BlockSpec((GDN_CONV, qw), lambda b, h, t: (0, kq + h)),
            pl.BlockSpec((GDN_CONV, vw), lambda b, h, t: (0, vq + h)),
            pl.BlockSpec((hp, 1, 2), lambda b, h, t: (h, 0, 0)),
            pl.BlockSpec((hp, 1, 2), lambda b, h, t: (h, 0, 0)),
            pl.BlockSpec((1, HEAD_DIM), lambda b, h, t: (0, 0)),
        ],
        out_specs=pl.BlockSpec((None, tb, vw), lambda b, h, t: (b, t, h)),
        scratch_shapes=[
            pltpu.VMEM((2 * hp, HEAD_DIM, HEAD_DIM), F32),
            pltpu.VMEM((tb + HALO, qw), F32),
            pltpu.VMEM((tb + HALO, qw), F32),
            pltpu.VMEM((tb + HALO, vw), F32),
        ],
        compiler_params=_cparams(("parallel", "parallel", "arbitrary")),
        name="gdn_core",
    )(proj3, proj3, proj3, proj3, rows, conv_w, conv_w, conv_w, alog2, dtb2,
      norm_w.reshape(1, HEAD_DIM))
    return out.reshape(batch * seq, VAL_DIM)


def _sconv_kernel(b_ref, c_ref, h_ref, cw_ref, o_ref, buf, *, tm):
    @pl.when(pl.program_id(2) == 0)
    def _():
        buf[0:HALO, :] = jnp.zeros((HALO, buf.shape[1]), F32)

    buf[HALO:HALO + tm, :] = c_ref[...].astype(F32) * h_ref[...].astype(F32)
    off = HALO - (SC_WIDTH - 1)
    acc = cw_ref[0:1, :] * buf[off:off + tm, :]
    for kk in range(1, SC_WIDTH):
        acc = acc + cw_ref[kk:kk + 1, :] * buf[off + kk:off + kk + tm, :]
    buf[0:HALO, :] = buf[tm:tm + HALO, :]
    o_ref[...] = (b_ref[...].astype(F32) * acc).astype(o_ref.dtype)


def _sconv(proj, conv_w, *, batch, seq, tm, tn):
    d = conv_w.shape[1]
    proj3 = proj.reshape(batch, seq, 3 * d)
    nj = d // tn
    out = pl.pallas_call(
        functools.partial(_sconv_kernel, tm=tm),
        out_shape=jax.ShapeDtypeStruct((batch, seq, d), BF16),
        grid=(batch, nj, seq // tm),
        in_specs=[
            pl.BlockSpec((None, tm, tn), lambda b, j, t: (b, t, j)),
            pl.BlockSpec((None, tm, tn), lambda b, j, t: (b, t, nj + j)),
            pl.BlockSpec((None, tm, tn), lambda b, j, t: (b, t, 2 * nj + j)),
            pl.BlockSpec((SC_WIDTH, tn), lambda b, j, t: (0, j)),
        ],
        out_specs=pl.BlockSpec((None, tm, tn), lambda b, j, t: (b, t, j)),
        scratch_shapes=[pltpu.VMEM((tm + HALO, tn), F32)],
        compiler_params=_cparams(("parallel", "parallel", "arbitrary")),
        name="sconv",
    )(proj3, proj3, proj3, conv_w)
    return out.reshape(batch * seq, d)


def _router_kernel(h_ref, g_ref, wr_ref, br_ref, xn_ref, code_ref, gate_ref, cnt_ref, carry_ref):
    @pl.when(pl.program_id(0) == 0)
    def _():
        carry_ref[...] = jnp.zeros_like(carry_ref)

    xn = _rms(h_ref[...], g_ref[...])
    xn_ref[...] = xn
    logits = _dot_split(xn, wr_ref[...]) + br_ref[...]
    tm = logits.shape[0]
    glog = logits[:, 0:N_GROUPS]
    elog = logits[:, N_GROUPS:N_GROUPS + N_EXPERTS]
    gl = lax.broadcasted_iota(jnp.int32, (tm, N_GROUPS), 1)
    gmax = jnp.max(glog, axis=-1, keepdims=True)
    group = jnp.min(jnp.where(glog == gmax, gl, N_GROUPS), axis=-1, keepdims=True)
    p_group = 1.0 / jnp.sum(jnp.exp(glog - gmax), axis=-1, keepdims=True)
    el = lax.broadcasted_iota(jnp.int32, (tm, N_EXPERTS), 1)
    neg = jnp.float32(-jnp.inf)
    within = jnp.where((el // EXPERTS_PER_GROUP) == group, elog, neg)
    m1 = jnp.max(within, axis=-1, keepdims=True)
    i1 = jnp.min(jnp.where(within == m1, el, N_EXPERTS), axis=-1, keepdims=True)
    rest = jnp.where(el == i1, neg, within)
    m2 = jnp.max(rest, axis=-1, keepdims=True)
    i2 = jnp.min(jnp.where(rest == m2, el, N_EXPERTS), axis=-1, keepdims=True)
    e2 = jnp.exp(m2 - m1)
    g1 = p_group / (1.0 + e2)
    g2 = p_group * e2 / (1.0 + e2)

    pick1, pick2 = el == i1, el == i2
    onehot = jnp.where(pick1 | pick2, 1.0, 0.0)
    rr = lax.broadcasted_iota(jnp.int32, (tm, tm), 0)
    cc = lax.broadcasted_iota(jnp.int32, (tm, tm), 1)
    before = jnp.where(cc < rr, 1.0, 0.0).astype(BF16)
    prefix = jnp.dot(before, onehot.astype(BF16), preferred_element_type=F32) + carry_ref[...]
    r1 = jnp.sum(jnp.where(pick1, prefix, 0.0), axis=-1, keepdims=True).astype(jnp.int32)
    r2 = jnp.sum(jnp.where(pick2, prefix, 0.0), axis=-1, keepdims=True).astype(jnp.int32)
    carry_ref[...] = carry_ref[...] + jnp.sum(onehot, axis=0, keepdims=True)
    cnt_ref[...] = carry_ref[...]

    lane2 = lax.broadcasted_iota(jnp.int32, (tm, 2), 1)
    code1 = jnp.left_shift(i1, RANK_BITS) | r1
    code2 = jnp.left_shift(i2, RANK_BITS) | r2
    code_ref[...] = jnp.where(lane2 == 0, code1, code2)
    gate_ref[...] = jnp.where(lane2 == 0, g1, g2)


def _router(h, gain, w_router, b_router, *, tm):
    t, d = h.shape
    nr = w_router.shape[1]
    return pl.pallas_call(
        _router_kernel,
        out_shape=(jax.ShapeDtypeStruct((t, d), F32),
                   jax.ShapeDtypeStruct((t, 2), jnp.int32),
                   jax.ShapeDtypeStruct((t, 2), F32),
                   jax.ShapeDtypeStruct((1, N_EXPERTS), F32)),
        grid=(t // tm,),
        in_specs=[
            pl.BlockSpec((tm, d), lambda i: (i, 0)),
            pl.BlockSpec((1, d), lambda i: (0, 0)),
            pl.BlockSpec((d, 2 * nr), lambda i: (0, 0)),
            pl.BlockSpec((1, nr), lambda i: (0, 0)),
        ],
        out_specs=(pl.BlockSpec((tm, d), lambda i: (i, 0)),
                   pl.BlockSpec((tm, 2), lambda i: (i, 0)),
                   pl.BlockSpec((tm, 2), lambda i: (i, 0)),
                   pl.BlockSpec((1, N_EXPERTS), lambda i: (0, 0))),
        scratch_shapes=[pltpu.VMEM((1, N_EXPERTS), F32)],
        compiler_params=_cparams(("arbitrary",)),
        name="moe_router",
    )(h, gain.reshape(1, d), _split2(w_router), b_router.reshape(1, nr))


def _dispatch_kernel(pf_ref, s0_ref, s1_ref, x_ref, xs_ref, zbuf, sem, zsem, *, tm, nb):
    base = pl.program_id(0) * tm

    @pl.when(pl.program_id(0) == 0)
    def _():
        zbuf[...] = jnp.zeros_like(zbuf)

        def zero_copy(b):
            return pltpu.make_async_copy(zbuf, xs_ref.at[pl.ds(b * MOE_BLOCK, MOE_BLOCK), :], zsem)

        def zstart(b, carry):
            @pl.when(pf_ref[b] != 0)
            def _():
                zero_copy(b).start()
            return carry

        def zwait(b, carry):
            @pl.when(pf_ref[b] != 0)
            def _():
                zero_copy(b).wait()
            return carry

        lax.fori_loop(0, nb, zstart, 0)
        lax.fori_loop(0, nb, zwait, 0)

    def row_copy(r, dest):
        return pltpu.make_async_copy(x_ref.at[pl.ds(r, 1), :], xs_ref.at[pl.ds(dest, 1), :], sem)

    def issue(r, carry):
        row_copy(r, s0_ref[base + r]).start()
        row_copy(r, s1_ref[base + r]).start()
        return carry

    lax.fori_loop(0, tm, issue, 0, unroll=8)
    whole = pltpu.make_async_copy(x_ref, xs_ref.at[pl.ds(0, tm), :], sem)
    whole.wait()
    whole.wait()


def _dispatch(xn, partial_block, slot0, slot1, cap, *, tm):
    t, d = xn.shape
    return pl.pallas_call(
        functools.partial(_dispatch_kernel, tm=tm, nb=cap // MOE_BLOCK),
        out_shape=jax.ShapeDtypeStruct((cap, d), F32),
        grid_spec=pltpu.PrefetchScalarGridSpec(
            num_scalar_prefetch=3,
            grid=(t // tm,),
            in_specs=[pl.BlockSpec((tm, d), lambda i, pf, s0, s1: (i, 0))],
            out_specs=pl.BlockSpec(memory_space=pl.ANY),
            scratch_shapes=[pltpu.VMEM((MOE_BLOCK, d), F32), pltpu.SemaphoreType.DMA,
                            pltpu.SemaphoreType.DMA],
        ),
        compiler_params=_cparams(("arbitrary",)),
        name="moe_dispatch",
    )(partial_block, slot0, slot1, xn)


def _expert_kernel(be_ref, nu_ref, x_ref, wgu_ref, wd_ref, o_ref, wgu_bf, wd_bf):
    i = pl.program_id(0)
    used = i < nu_ref[0]
    new_expert = (i == 0) | (be_ref[i] != be_ref[jnp.maximum(i - 1, 0)])

    @pl.when(used & new_expert)
    def _():
        wgu_bf[...] = wgu_ref[...].astype(BF16)
        wd_bf[...] = wd_ref[...].astype(BF16)

    @pl.when(used)
    def _():
        x = x_ref[...].astype(BF16)
        gu = jnp.dot(x, wgu_bf[...], preferred_element_type=F32)
        gt, up = gu[:, :D_EXPERT], gu[:, D_EXPERT:]
        act = (gt * _sigmoid(gt) * up).astype(BF16)
        o_ref[...] = jnp.dot(act, wd_bf[...], preferred_element_type=F32)

    @pl.when(jnp.logical_not(used))
    def _():
        o_ref[...] = jnp.zeros_like(o_ref)


def _experts(xs, w_gu, w_down, layer, block_expert, n_used):
    cap, d = xs.shape
    nb = cap // MOE_BLOCK

    def blk(i, be, nu):
        return jnp.maximum(jnp.minimum(i, nu[0] - 1), 0)

    return pl.pallas_call(
        _expert_kernel,
        out_shape=jax.ShapeDtypeStruct((cap, d), F32),
        grid_spec=pltpu.PrefetchScalarGridSpec(
            num_scalar_prefetch=2,
            grid=(nb,),
            in_specs=[
                pl.BlockSpec((MOE_BLOCK, d), lambda i, be, nu: (blk(i, be, nu), 0)),
                pl.BlockSpec((None, None, d, 2 * D_EXPERT),
                             lambda i, be, nu: (layer, be[blk(i, be, nu)], 0, 0)),
                pl.BlockSpec((None, None, D_EXPERT, d),
                             lambda i, be, nu: (layer, be[blk(i, be, nu)], 0, 0)),
            ],
            out_specs=pl.BlockSpec((MOE_BLOCK, d), lambda i, be, nu: (i, 0)),
            scratch_shapes=[pltpu.VMEM((d, 2 * D_EXPERT), BF16), pltpu.VMEM((D_EXPERT, d), BF16)],
        ),
        compiler_params=_cparams(("arbitrary",)),
        name="moe_experts",
    )(block_expert, n_used, xs, w_gu, w_down)


def _combine_kernel(s0_ref, s1_ref, h_ref, gate_ref, fg_ref, ys_ref, o_ref, buf, sem, *,
                    tm, final):
    base = pl.program_id(0) * tm

    def row_copy(r, k, src):
        return pltpu.make_async_copy(ys_ref.at[pl.ds(src, 1), :], buf.at[k, pl.ds(r, 1), :], sem)

    def issue(r, carry):
        row_copy(r, 0, s0_ref[base + r]).start()
        row_copy(r, 1, s1_ref[base + r]).start()
        return carry

    lax.fori_loop(0, tm, issue, 0, unroll=8)
    for k in range(2):
        pltpu.make_async_copy(ys_ref.at[pl.ds(0, tm), :], buf.at[k], sem).wait()
    gate = gate_ref[...]
    y = h_ref[...] + (gate[:, 0:1] * buf[0] + gate[:, 1:2] * buf[1])
    if final:
        y = _rms(y, fg_ref[...])
    o_ref[...] = y


def _combine(h, gates, slot0, slot1, ys, final_gain, *, tm, final):
    t, d = h.shape
    return pl.pallas_call(
        functools.partial(_combine_kernel, tm=tm, final=final),
        out_shape=jax.ShapeDtypeStruct((t, d), F32),
        grid_spec=pltpu.PrefetchScalarGridSpec(
            num_scalar_prefetch=2,
            grid=(t // tm,),
            in_specs=[
                pl.BlockSpec((tm, d), lambda i, s0, s1: (i, 0)),
                pl.BlockSpec((tm, 2), lambda i, s0, s1: (i, 0)),
                pl.BlockSpec((1, d), lambda i, s0, s1: (0, 0)),
                pl.BlockSpec(memory_space=pl.ANY),
            ],
            out_specs=pl.BlockSpec((tm, d), lambda i, s0, s1: (i, 0)),
            scratch_shapes=[pltpu.VMEM((2, tm, d), F32), pltpu.SemaphoreType.DMA],
        ),
        compiler_params=_cparams(("arbitrary",)),
        name="moe_combine",
    )(slot0, slot1, h, gates, final_gain.reshape(1, d), ys)


def _block_table(counts, t):
    counts = counts.astype(jnp.int32)
    padded = (counts + MOE_BLOCK - 1) // MOE_BLOCK * MOE_BLOCK
    pend = jnp.cumsum(padded)
    pstart = (pend - padded).astype(jnp.int32)
    n_blocks = (2 * t + MOE_BLOCK - 1) // MOE_BLOCK + N_EXPERTS
    block_start = jnp.arange(n_blocks, dtype=jnp.int32) * MOE_BLOCK
    block_expert = jnp.minimum(jnp.sum((pend[None, :] <= block_start[:, None]).astype(jnp.int32), axis=1),
                               N_EXPERTS - 1).astype(jnp.int32)
    n_used = (pend[-1] // MOE_BLOCK).astype(jnp.int32).reshape(1)
    filled_to = (pstart + counts)[block_expert]
    partial_block = ((block_start + MOE_BLOCK > filled_to) | (block_start >= pend[-1])).astype(jnp.int32)
    return pstart, block_expert, n_used, partial_block, n_blocks * MOE_BLOCK


def _moe(h, gain, w_group, b_group, w_expert, b_expert, w_gu, w_down, layer, final_gain, *,
         final, tm):
    t, _ = h.shape
    pad = ROUTER_WIDTH - N_GROUPS - N_EXPERTS
    w_router = jnp.concatenate([w_group, w_expert, jnp.zeros((w_group.shape[0], pad), F32)], axis=1)
    b_router = jnp.concatenate([b_group, b_expert, jnp.zeros((pad,), F32)], axis=0)
    xn, code, gates, counts = _router(h, gain, w_router, b_router, tm=tm)
    pstart, block_expert, n_used, partial_block, cap = _block_table(counts[0], t)
    expert = lax.shift_right_logical(code, RANK_BITS)
    onehot = expert[:, :, None] == jnp.arange(N_EXPERTS, dtype=jnp.int32)
    slot = jnp.sum(jnp.where(onehot, pstart, 0), axis=-1) + (code & ((1 << RANK_BITS) - 1))
    slot0, slot1 = slot[:, 0], slot[:, 1]
    xs = _dispatch(xn, partial_block, slot0, slot1, cap, tm=tm)
    ys = _experts(xs, w_gu, w_down, layer, block_expert, n_used)
    return _combine(h, gates, slot0, slot1, ys, final_gain, tm=tm, final=final)


def _pick(n, pref):
    for c in pref:
        if n % c == 0:
            return c
    return n


def kernel(x, norm_mix, norm_ffn, gdn_w_in, gdn_conv_w, gdn_a_log, gdn_dt_bias, gdn_norm_w,
           gdn_w_out, sc_w_in, sc_conv_w, sc_w_out, moe_w_group, moe_b_group, moe_w_expert,
           moe_b_expert, moe_w_gu, moe_w_down, norm_final):
    batch, seq, d = x.shape
    t = batch * seq
    h = x.reshape(t, d)
    tm = _pick(t, (1024, 512, 256))
    tb = _pick(seq, (256, 128, 64))

    w_in = gdn_w_in[0]
    proj, ba = _norm_matmul_side(h, norm_mix[0], w_in[:, :MAIN_DIM].astype(BF16),
                                 w_in[:, MAIN_DIM:], tm=tm, tn=1024, out_dtype=F32)
    o = _gdn_core(proj, ba, gdn_conv_w[0], gdn_a_log[0], gdn_dt_bias[0], gdn_norm_w[0],
                  batch=batch, seq=seq, tb=tb, hp=GDN_HEADS_PER_STEP)
    h = _matmul_res(o, gdn_w_out[0].astype(BF16), h, tm=tm, tn=1024)
    h = _moe(h, norm_ffn[0], moe_w_group[0], moe_b_group[0], moe_w_expert[0], moe_b_expert[0],
             moe_w_gu, moe_w_down, 0, norm_final, final=False, tm=_pick(t, (256,)))

    proj = _norm_matmul(h, norm_mix[1], sc_w_in[0].astype(BF16), tm=tm, tn=1024, out_dtype=F32)
    y = _sconv(proj, sc_conv_w[0], batch=batch, seq=seq, tm=_pick(seq, (512, 256)), tn=512)
    h = _matmul_res(y, sc_w_out[0].astype(BF16), h, tm=tm, tn=1024)
    h = _moe(h, norm_ffn[1], moe_w_group[1], moe_b_group[1], moe_w_expert[1], moe_b_expert[1],
             moe_w_gu, moe_w_down, 1, norm_final, final=True, tm=_pick(t, (256,)))
    return h.reshape(batch, seq, d)
```

```python
import functools

import jax
import jax.numpy as jnp
from jax import lax
from jax.experimental import pallas as pl
from jax.experimental.pallas import tpu as pltpu

EPS = 1e-6
F32 = jnp.float32
BF16 = jnp.bfloat16

QK_HEADS = 16
V_HEADS = 32
HEAD_DIM = 128
KEY_DIM = QK_HEADS * HEAD_DIM
VAL_DIM = V_HEADS * HEAD_DIM
QKV_DIM = 2 * KEY_DIM + VAL_DIM
MAIN_DIM = QKV_DIM + VAL_DIM
GDN_CONV = 4
CHUNK = 64
HALO = 8
GDN_HEADS_PER_STEP = 4
GATE_ROWS = 16

SC_WIDTH = 3

N_GROUPS = 8
EXPERTS_PER_GROUP = 8
N_EXPERTS = N_GROUPS * EXPERTS_PER_GROUP
D_EXPERT = 512
MOE_BLOCK = 512
ROUTER_WIDTH = 128
RANK_BITS = 16

VMEM_LIMIT = 56 * 1024 * 1024


def _cparams(sem):
    return pltpu.CompilerParams(dimension_semantics=sem, vmem_limit_bytes=VMEM_LIMIT)


def _rms(x, gain):
    return x * lax.rsqrt(jnp.mean(x * x, axis=-1, keepdims=True) + EPS) * gain


def _sigmoid(x):
    return 0.5 * jnp.tanh(0.5 * x) + 0.5


def _softplus(x):
    return jnp.maximum(x, 0.0) + jnp.log1p(jnp.exp(-jnp.abs(x)))


def _pack_rows(v):
    half = v.shape[1] // 2
    bits = pltpu.bitcast(v, jnp.uint32)
    rounded = bits + jnp.uint32(0x7FFF) + ((bits >> 16) & jnp.uint32(1))
    return (rounded[:, :half] >> 16) | (rounded[:, half:] & jnp.uint32(0xFFFF0000))


def _unpack_rows(p):
    lo = pltpu.bitcast(p << 16, F32)
    hi = pltpu.bitcast(p & jnp.uint32(0xFFFF0000), F32)
    return lo, hi


def _split2(w):
    hi = w.astype(BF16)
    lo = (w - hi.astype(F32)).astype(BF16)
    return jnp.concatenate([hi, lo], axis=-1)


def _dot_split(x, w2):
    n = w2.shape[-1] // 2
    x_hi = x.astype(BF16)
    x_lo = (x - x_hi.astype(F32)).astype(BF16)
    a = jnp.dot(x_hi, w2, preferred_element_type=F32)
    b = jnp.dot(x_lo, w2[:, :n], preferred_element_type=F32)
    return (a[:, :n] + a[:, n:]) + b


def _norm_matmul_side_kernel(x_ref, g_ref, w_ref, ws_ref, o_ref, os_ref, xn_ref):
    @pl.when(pl.program_id(1) == 0)
    def _():
        xn = _rms(x_ref[...], g_ref[...])
        xn_ref[...] = xn.astype(BF16)
        os_ref[...] = _dot_split(xn, ws_ref[...])

    o_ref[...] = jnp.dot(xn_ref[...], w_ref[...], preferred_element_type=F32).astype(o_ref.dtype)


def _norm_matmul_side(x, gain, w, w_side, *, tm, tn, out_dtype):
    t, k = x.shape
    n, ns = w.shape[1], w_side.shape[1]
    return pl.pallas_call(
        _norm_matmul_side_kernel,
        out_shape=(jax.ShapeDtypeStruct((t, n), out_dtype), jax.ShapeDtypeStruct((t, ns), F32)),
        grid=(t // tm, n // tn),
        in_specs=[
            pl.BlockSpec((tm, k), lambda i, j: (i, 0)),
            pl.BlockSpec((1, k), lambda i, j: (0, 0)),
            pl.BlockSpec((k, tn), lambda i, j: (0, j)),
            pl.BlockSpec((k, 2 * ns), lambda i, j: (0, 0)),
        ],
        out_specs=(pl.BlockSpec((tm, tn), lambda i, j: (i, j)),
                   pl.BlockSpec((tm, ns), lambda i, j: (i, 0))),
        scratch_shapes=[pltpu.VMEM((tm, k), BF16)],
        compiler_params=_cparams(("parallel", "arbitrary")),
        name="norm_matmul_side",
    )(x, gain.reshape(1, k), w, _split2(w_side))


def _matmul_res_kernel(a_ref, w_ref, r_ref, o_ref):
    o_ref[...] = r_ref[...] + jnp.dot(a_ref[...], w_ref[...], preferred_element_type=F32)


def _matmul_res(a, w, res, *, tm, tn):
    t, k = a.shape
    n = w.shape[1]
    return pl.pallas_call(
        _matmul_res_kernel,
        out_shape=jax.ShapeDtypeStruct((t, n), F32),
        grid=(t // tm, n // tn),
        in_specs=[
            pl.BlockSpec((tm, k), lambda i, j: (i, 0)),
            pl.BlockSpec((k, tn), lambda i, j: (0, j)),
            pl.BlockSpec((tm, tn), lambda i, j: (i, j)),
        ],
        out_specs=pl.BlockSpec((tm, tn), lambda i, j: (i, j)),
        compiler_params=_cparams(("parallel", "arbitrary")),
        name="matmul_res",
    )(a, w, res)


def _gdn_kernel(q_ref, k_ref, v_ref, z_ref, row_ref, cwq_ref, cwk_ref, cwv_ref,
                alog_ref, dtb_ref, nw_ref, o_ref, state_ref, qbuf, kbuf, vbuf, *, tb, hp):
    nchunk = tb // CHUNK

    @pl.when(pl.program_id(2) == 0)
    def _():
        state_ref[...] = jnp.zeros_like(state_ref)
        qbuf[0:HALO, :] = jnp.zeros((HALO, qbuf.shape[1]), F32)
        kbuf[0:HALO, :] = jnp.zeros((HALO, kbuf.shape[1]), F32)
        vbuf[0:HALO, :] = jnp.zeros((HALO, vbuf.shape[1]), F32)

    def conv_silu(x_ref, buf, cw_ref):
        buf[HALO:HALO + tb, :] = x_ref[...].astype(F32)
        xb = buf[...]
        acc = cw_ref[0:1, :] * xb
        for kk in range(1, GDN_CONV):
            acc = cw_ref[kk:kk + 1, :] * xb + pltpu.roll(acc, 1, axis=0)
        buf[0:HALO, :] = buf[tb:tb + HALO, :]
        acc = acc[HALO:, :]
        return acc * _sigmoid(acc)

    def l2norm(x):
        return x * lax.rsqrt(jnp.sum(x * x, axis=-1, keepdims=True) + EPS)

    q_all = conv_silu(q_ref, qbuf, cwq_ref)
    k_all = conv_silu(k_ref, kbuf, cwk_ref)
    v_all = conv_silu(v_ref, vbuf, cwv_ref)

    ri = lax.broadcasted_iota(jnp.int32, (tb, tb), 0)
    ci = lax.broadcasted_iota(jnp.int32, (tb, tb), 1)
    same_chunk = (ri // CHUNK) == (ci // CHUNK)
    cum_tot = jnp.concatenate([jnp.where(same_chunk & (ri <= ci), 1.0, 0.0),
                               jnp.where(same_chunk, 1.0, 0.0)], axis=1).astype(BF16)
    si = lax.broadcasted_iota(jnp.int32, (3 * GATE_ROWS, GATE_ROWS), 0)
    sj = lax.broadcasted_iota(jnp.int32, (3 * GATE_ROWS, GATE_ROWS), 1)
    fold3 = jnp.where(si % GATE_ROWS == sj, 1.0, 0.0).astype(BF16)

    def split3(x):
        hi = x.astype(BF16).astype(F32)
        r1 = x - hi
        mid = r1.astype(BF16).astype(F32)
        lo = r1 - mid
        return jnp.concatenate([hi, mid, lo], axis=0).astype(BF16)

    r64 = lax.broadcasted_iota(jnp.int32, (CHUNK, CHUNK), 0)
    c64 = lax.broadcasted_iota(jnp.int32, (CHUNK, CHUNK), 1)
    tril = c64 <= r64
    strict = c64 < r64
    eye = jnp.where(c64 == r64, 1.0, 0.0).astype(F32)

    def mm(a, b):
        return lax.dot_general(a, b.astype(BF16), (((1,), (0,)), ((), ())),
                               preferred_element_type=F32)

    def mm_nt(a, b):
        return lax.dot_general(a.astype(BF16), b.astype(BF16), (((1,), (1,)), ((), ())),
                               preferred_element_type=F32)

    def mm_tn(a, b):
        return lax.dot_general(a.astype(BF16), b.astype(BF16), (((0,), (0,)), ((), ())),
                               preferred_element_type=F32)

    chunks = range(nchunk)
    rows = [slice(c * CHUNK, (c + 1) * CHUNK) for c in chunks]
    pairs = [(p, c) for p in range(hp) for c in chunks]
    heads = [(p, c, j) for p in range(hp) for c in chunks for j in range(2)]

    qc, kc, gate_col, gc_row = {}, {}, {}, {}
    for p in range(hp):
        hl = slice(p * HEAD_DIM, (p + 1) * HEAD_DIM)
        q = l2norm(q_all[:, hl]) * (HEAD_DIM ** -0.5)
        k = l2norm(k_all[:, hl])
        for c in chunks:
            qc[p, c], kc[p, c] = q[rows[c]], k[rows[c]]
        alog, dtb = alog_ref[p].reshape(2, 1), dtb_ref[p].reshape(2, 1)
        beta_r = _sigmoid(row_ref[p, 0:2, :])
        g_r = -jnp.exp(alog) * _softplus(row_ref[p, 2:4, :] + dtb)
        r3 = jnp.dot(split3(g_r), cum_tot, preferred_element_type=F32)
        r3 = (r3[0:2, :] + r3[2:4, :]) + r3[4:6, :]
        gc_r, gtot_r = r3[:, :tb], r3[:, tb:]
        egc_r = jnp.exp(gc_r)
        gc_row[p] = gc_r
        gate_rows = jnp.concatenate(
            [beta_r, gc_r, egc_r, jnp.exp(gtot_r - gc_r), jnp.exp(gtot_r), beta_r * egc_r,
             jnp.zeros((GATE_ROWS - 12, tb), F32)], axis=0)
        gate_col[p] = lax.dot_general(split3(gate_rows), fold3, (((0,), (0,)), ((), ())),
                                      preferred_element_type=F32)

    kk_t = {pc: mm_nt(kc[pc], kc[pc]) for pc in pairs}
    qk_t = {pc: mm_nt(qc[pc], kc[pc]) for pc in pairs}

    beta, egc, kdec, etot, bege, lower, attn = {}, {}, {}, {}, {}, {}, {}
    for h in heads:
        p, c, j = h
        col = lambda i: gate_col[p][rows[c], i + j:i + j + 1]
        beta[h], gcc, egc[h], kdec[h], bege[h] = col(0), col(2), col(4), col(6), col(10)
        etot[h] = col(8)[0:1, :]
        gcr = gc_row[p][j:j + 1, rows[c]]
        decay = jnp.exp(jnp.where(tril, gcc - gcr, -jnp.inf))
        lower[h] = jnp.where(strict, beta[h] * kk_t[p, c] * decay, 0.0)
        attn[h] = jnp.where(tril, qk_t[p, c] * decay, 0.0)

    inv = {h: eye - lower[h] for h in heads}
    power = dict(lower)
    for _ in range(5):
        power = {h: mm(power[h], power[h]) for h in heads}
        inv = {h: inv[h] + mm(inv[h], power[h]) for h in heads}

    uw = {}
    for h in heads:
        p, c, j = h
        vl = slice((2 * p + j) * HEAD_DIM, (2 * p + j + 1) * HEAD_DIM)
        rhs = jnp.concatenate([v_all[rows[c], vl] * beta[h], kc[p, c] * bege[h]], axis=1)
        uw[h] = mm(inv[h], rhs)
    nk = {h: mm_tn(kc[h[0], h[1]] * kdec[h], uw[h]) for h in heads}
    ao = {h: mm(attn[h], uw[h]) for h in heads}
    qp = {h: (qc[h[0], h[1]] * egc[h] - ao[h][:, HEAD_DIM:]).astype(BF16) for h in heads}

    state = {(p, j): state_ref[2 * p + j] for p in range(hp) for j in range(2)}
    seen = {}
    for c in chunks:
        for p in range(hp):
            for j in range(2):
                h = (p, c, j)
                s_bf = state[p, j].astype(BF16)
                seen[h] = s_bf
                n_c, k_c = nk[h][:, :HEAD_DIM], nk[h][:, HEAD_DIM:]
                state[p, j] = state[p, j] * etot[h] + (n_c - mm(k_c, s_bf))
    for p in range(hp):
        for j in range(2):
            state_ref[2 * p + j] = state[p, j]

    nw = nw_ref[...]
    for h in heads:
        p, c, j = h
        vl = slice((2 * p + j) * HEAD_DIM, (2 * p + j + 1) * HEAD_DIM)
        o = jnp.dot(qp[h], seen[h], preferred_element_type=F32) + ao[h][:, :HEAD_DIM]
        zc = z_ref[rows[c], vl].astype(F32)
        o = o * lax.rsqrt(jnp.mean(o * o, axis=-1, keepdims=True) + EPS) * nw
        o_ref[rows[c], vl] = (o * (zc * _sigmoid(zc))).astype(o_ref.dtype)


def _gdn_core(proj, ba, conv_w, a_log, dt_bias, norm_w, *, batch, seq, tb, hp):
    proj3 = proj.reshape(batch, seq, MAIN_DIM)
    ba4 = ba.reshape(batch, seq, 2, QK_HEADS, 2)
    rows = jnp.transpose(ba4, (0, 3, 2, 4, 1)).reshape(batch, QK_HEADS, 4, seq)
    alog2 = a_log.reshape(QK_HEADS, 1, 2)
    dtb2 = dt_bias.reshape(QK_HEADS, 1, 2)
    qw, vw = hp * HEAD_DIM, 2 * hp * HEAD_DIM
    kq = KEY_DIM // qw
    vq = (2 * KEY_DIM) // vw
    zq = QKV_DIM // vw
    out = pl.pallas_call(
        functools.partial(_gdn_kernel, tb=tb, hp=hp),
        out_shape=jax.ShapeDtypeStruct((batch, seq, VAL_DIM), BF16),
        grid=(batch, QK_HEADS // hp, seq // tb),
        in_specs=[
            pl.BlockSpec((None, tb, qw), lambda b, h, t: (b, t, h)),
            pl.BlockSpec((None, tb, qw), lambda b, h, t: (b, t, kq + h)),
            pl.BlockSpec((None, tb, vw), lambda b, h, t: (b, t, vq + h)),
            pl.BlockSpec((None, tb, vw), lambda b, h, t: (b, t, zq + h)),
            pl.BlockSpec((None, hp, 4, tb), lambda b, h, t: (b, h, 0, t)),
            pl.BlockSpec((GDN_CONV, qw), lambda b, h, t: (0, h)),
            pl.BlockSpec((GDN_CONV, qw), lambda b, h, t: (0, kq + h)),
            pl.BlockSpec((GDN_CONV, vw), lambda b, h, t: (0, vq + h)),
            pl.BlockSpec((hp, 1, 2), lambda b, h, t: (h, 0, 0)),
            pl.BlockSpec((hp, 1, 2), lambda b, h, t: (h, 0, 0)),
            pl.BlockSpec((1, HEAD_DIM), lambda b, h, t: (0, 0)),
        ],
        out_specs=pl.BlockSpec((None, tb, vw), lambda b, h, t: (b, t, h)),
        scratch_shapes=[
            pltpu.VMEM((2 * hp, HEAD_DIM, HEAD_DIM), F32),
            pltpu.VMEM((tb + HALO, qw), F32),
            pltpu.VMEM((tb + HALO, qw), F32),
            pltpu.VMEM((tb + HALO, vw), F32),
        ],
        compiler_params=_cparams(("parallel", "parallel", "arbitrary")),
        name="gdn_core",
    )(proj3, proj3, proj3, proj3, rows, conv_w, conv_w, conv_w, alog2, dtb2,
      norm_w.reshape(1, HEAD_DIM))
    return out.reshape(batch * seq, VAL_DIM)


def _sconv_proj_kernel(x_ref, g_ref, wb_ref, wc_ref, wh_ref, cw_ref, o_ref, xn_ref, halo_ref, buf,
                       *, tm, tiles_per_seq):
    i, j = pl.program_id(0), pl.program_id(1)

    @pl.when(j == 0)
    def _():
        xn_ref[...] = _rms(x_ref[...], g_ref[...]).astype(BF16)

    @pl.when(i % tiles_per_seq == 0)
    def _():
        halo_ref[j] = jnp.zeros((HALO, halo_ref.shape[2]), F32)

    xn = xn_ref[...]
    gate_b = jnp.dot(xn, wb_ref[...], preferred_element_type=F32)
    u = (jnp.dot(xn, wc_ref[...], preferred_element_type=F32)
         * jnp.dot(xn, wh_ref[...], preferred_element_type=F32))
    buf[0:HALO, :] = halo_ref[j]
    buf[HALO:HALO + tm, :] = u
    halo_ref[j] = u[tm - HALO:tm, :]
    ub = buf[...]
    acc = cw_ref[0:1, :] * ub
    for kk in range(1, SC_WIDTH):
        acc = cw_ref[kk:kk + 1, :] * ub + pltpu.roll(acc, 1, axis=0)
    o_ref[...] = (gate_b * acc[HALO:, :]).astype(o_ref.dtype)


def _sconv_proj(x, gain, w_in, conv_w, *, seq, tm, tn):
    t, k = x.shape
    d = conv_w.shape[1]
    nj = d // tn
    return pl.pallas_call(
        functools.partial(_sconv_proj_kernel, tm=tm, tiles_per_seq=seq // tm),
        out_shape=jax.ShapeDtypeStruct((t, d), BF16),
        grid=(t // tm, nj),
        in_specs=[
            pl.BlockSpec((tm, k), lambda i, j: (i, 0)),
            pl.BlockSpec((1, k), lambda i, j: (0, 0)),
            pl.BlockSpec((k, tn), lambda i, j: (0, j)),
            pl.BlockSpec((k, tn), lambda i, j: (0, nj + j)),
            pl.BlockSpec((k, tn), lambda i, j: (0, 2 * nj + j)),
            pl.BlockSpec((SC_WIDTH, tn), lambda i, j: (0, j)),
        ],
        out_specs=pl.BlockSpec((tm, tn), lambda i, j: (i, j)),
        scratch_shapes=[pltpu.VMEM((tm, k), BF16), pltpu.VMEM((nj, HALO, tn), F32),
                        pltpu.VMEM((tm + HALO, tn), F32)],
        compiler_params=_cparams(("arbitrary", "arbitrary")),
        name="sconv_proj",
    )(x, gain.reshape(1, k), w_in, w_in, w_in, conv_w)


def _router_kernel(h_ref, g_ref, wr_ref, br_ref, xn_ref, code_ref, gate_ref, cnt_ref, carry_ref):
    @pl.when(pl.program_id(0) == 0)
    def _():
        carry_ref[...] = jnp.zeros_like(carry_ref)

    xn = _rms(h_ref[...], g_ref[...])
    xn_ref[...] = _pack_rows(xn)
    logits = _dot_split(xn, wr_ref[...]) + br_ref[...]
    tm = logits.shape[0]
    glog = logits[:, 0:N_GROUPS]
    elog = logits[:, N_GROUPS:N_GROUPS + N_EXPERTS]
    gl = lax.broadcasted_iota(jnp.int32, (tm, N_GROUPS), 1)
    gmax = jnp.max(glog, axis=-1, keepdims=True)
    group = jnp.min(jnp.where(glog == gmax, gl, N_GROUPS), axis=-1, keepdims=True)
    p_group = 1.0 / jnp.sum(jnp.exp(glog - gmax), axis=-1, keepdims=True)
    el = lax.broadcasted_iota(jnp.int32, (tm, N_EXPERTS), 1)
    neg = jnp.float32(-jnp.inf)
    within = jnp.where((el // EXPERTS_PER_GROUP) == group, elog, neg)
    m1 = jnp.max(within, axis=-1, keepdims=True)
    i1 = jnp.min(jnp.where(within == m1, el, N_EXPERTS), axis=-1, keepdims=True)
    rest = jnp.where(el == i1, neg, within)
    m2 = jnp.max(rest, axis=-1, keepdims=True)
    i2 = jnp.min(jnp.where(rest == m2, el, N_EXPERTS), axis=-1, keepdims=True)
    e2 = jnp.exp(m2 - m1)
    g1 = p_group / (1.0 + e2)
    g2 = p_group * e2 / (1.0 + e2)

    pick1, pick2 = el == i1, el == i2
    onehot = jnp.where(pick1 | pick2, 1.0, 0.0)
    rr = lax.broadcasted_iota(jnp.int32, (tm, tm), 0)
    cc = lax.broadcasted_iota(jnp.int32, (tm, tm), 1)
    before = jnp.where(cc < rr, 1.0, 0.0).astype(BF16)
    prefix = jnp.dot(before, onehot.astype(BF16), preferred_element_type=F32) + carry_ref[...]
    r1 = jnp.sum(jnp.where(pick1, prefix, 0.0), axis=-1, keepdims=True).astype(jnp.int32)
    r2 = jnp.sum(jnp.where(pick2, prefix, 0.0), axis=-1, keepdims=True).astype(jnp.int32)
    carry_ref[...] = carry_ref[...] + jnp.sum(onehot, axis=0, keepdims=True)
    cnt_ref[...] = carry_ref[...]

    lane2 = lax.broadcasted_iota(jnp.int32, (tm, 2), 1)
    code1 = jnp.left_shift(i1, RANK_BITS) | r1
    code2 = jnp.left_shift(i2, RANK_BITS) | r2
    code_ref[...] = jnp.where(lane2 == 0, code1, code2)
    gate_ref[...] = jnp.where(lane2 == 0, g1, g2)


def _router(h, gain, w_router, b_router, *, tm):
    t, d = h.shape
    nr = w_router.shape[1]
    return pl.pallas_call(
        _router_kernel,
        out_shape=(jax.ShapeDtypeStruct((t, d // 2), jnp.uint32),
                   jax.ShapeDtypeStruct((t, 2), jnp.int32),
                   jax.ShapeDtypeStruct((t, 2), F32),
                   jax.ShapeDtypeStruct((1, N_EXPERTS), F32)),
        grid=(t // tm,),
        in_specs=[
            pl.BlockSpec((tm, d), lambda i: (i, 0)),
            pl.BlockSpec((1, d), lambda i: (0, 0)),
            pl.BlockSpec((d, 2 * nr), lambda i: (0, 0)),
            pl.BlockSpec((1, nr), lambda i: (0, 0)),
        ],
        out_specs=(pl.BlockSpec((tm, d // 2), lambda i: (i, 0)),
                   pl.BlockSpec((tm, 2), lambda i: (i, 0)),
                   pl.BlockSpec((tm, 2), lambda i: (i, 0)),
                   pl.BlockSpec((1, N_EXPERTS), lambda i: (0, 0))),
        scratch_shapes=[pltpu.VMEM((1, N_EXPERTS), F32)],
        compiler_params=_cparams(("arbitrary",)),
        name="moe_router",
    )(h, gain.reshape(1, d), _split2(w_router), b_router.reshape(1, nr))


def _dispatch_kernel(pf_ref, s0_ref, s1_ref, x_ref, xs_ref, zbuf, sem, zsem, *, tm, nb):
    base = pl.program_id(0) * tm

    @pl.when(pl.program_id(0) == 0)
    def _():
        zbuf[...] = jnp.zeros_like(zbuf)

        def zero_copy(b):
            return pltpu.make_async_copy(zbuf, xs_ref.at[pl.ds(b * MOE_BLOCK, MOE_BLOCK), :], zsem)

        def zstart(b, carry):
            @pl.when(pf_ref[b] != 0)
            def _():
                zero_copy(b).start()
            return carry

        def zwait(b, carry):
            @pl.when(pf_ref[b] != 0)
            def _():
                zero_copy(b).wait()
            return carry

        lax.fori_loop(0, nb, zstart, 0)
        lax.fori_loop(0, nb, zwait, 0)

    def row_copy(r, dest):
        return pltpu.make_async_copy(x_ref.at[pl.ds(r, 1), :], xs_ref.at[pl.ds(dest, 1), :], sem)

    def issue(r, carry):
        row_copy(r, s0_ref[base + r]).start()
        row_copy(r, s1_ref[base + r]).start()
        return carry

    lax.fori_loop(0, tm, issue, 0, unroll=8)
    whole = pltpu.make_async_copy(x_ref, xs_ref.at[pl.ds(0, tm), :], sem)
    whole.wait()
    whole.wait()


def _dispatch(xn, partial_block, slot0, slot1, cap, *, tm):
    t, d = xn.shape
    return pl.pallas_call(
        functools.partial(_dispatch_kernel, tm=tm, nb=cap // MOE_BLOCK),
        out_shape=jax.ShapeDtypeStruct((cap, d), xn.dtype),
        grid_spec=pltpu.PrefetchScalarGridSpec(
            num_scalar_prefetch=3,
            grid=(t // tm,),
            in_specs=[pl.BlockSpec((tm, d), lambda i, pf, s0, s1: (i, 0))],
            out_specs=pl.BlockSpec(memory_space=pl.ANY),
            scratch_shapes=[pltpu.VMEM((MOE_BLOCK, d), xn.dtype), pltpu.SemaphoreType.DMA,
                            pltpu.SemaphoreType.DMA],
        ),
        compiler_params=_cparams(("arbitrary",)),
        name="moe_dispatch",
    )(partial_block, slot0, slot1, xn)


def _expert_kernel(be_ref, nu_ref, x_ref, wgu_ref, wd_ref, o_ref, wgu_bf, wd_bf):
    i = pl.program_id(0)
    used = i < nu_ref[0]
    new_expert = (i == 0) | (be_ref[i] != be_ref[jnp.maximum(i - 1, 0)])

    @pl.when(used & new_expert)
    def _():
        wgu_bf[...] = wgu_ref[...].astype(BF16)
        wd_bf[...] = wd_ref[...].astype(BF16)

    @pl.when(used)
    def _():
        x_lo, x_hi = _unpack_rows(x_ref[...])
        half = x_lo.shape[1]
        gu = (jnp.dot(x_lo.astype(BF16), wgu_bf[0:half, :], preferred_element_type=F32)
              + jnp.dot(x_hi.astype(BF16), wgu_bf[half:, :], preferred_element_type=F32))
        gt, up = gu[:, :D_EXPERT], gu[:, D_EXPERT:]
        act = (gt * _sigmoid(gt) * up).astype(BF16)
        o_ref[...] = _pack_rows(jnp.dot(act, wd_bf[...], preferred_element_type=F32))

    @pl.when(jnp.logical_not(used))
    def _():
        o_ref[...] = jnp.zeros_like(o_ref)


def _experts(xs, w_gu, w_down, layer, block_expert, n_used):
    cap, dp = xs.shape
    d = 2 * dp
    nb = cap // MOE_BLOCK

    def blk(i, be, nu):
        return jnp.maximum(jnp.minimum(i, nu[0] - 1), 0)

    return pl.pallas_call(
        _expert_kernel,
        out_shape=jax.ShapeDtypeStruct((cap, dp), jnp.uint32),
        grid_spec=pltpu.PrefetchScalarGridSpec(
            num_scalar_prefetch=2,
            grid=(nb,),
            in_specs=[
                pl.BlockSpec((MOE_BLOCK, dp), lambda i, be, nu: (blk(i, be, nu), 0)),
                pl.BlockSpec((None, None, d, 2 * D_EXPERT),
                             lambda i, be, nu: (layer, be[blk(i, be, nu)], 0, 0)),
                pl.BlockSpec((None, None, D_EXPERT, d),
                             lambda i, be, nu: (layer, be[blk(i, be, nu)], 0, 0)),
            ],
            out_specs=pl.BlockSpec((MOE_BLOCK, dp), lambda i, be, nu: (i, 0)),
            scratch_shapes=[pltpu.VMEM((d, 2 * D_EXPERT), BF16), pltpu.VMEM((D_EXPERT, d), BF16)],
        ),
        compiler_params=_cparams(("arbitrary",)),
        name="moe_experts",
    )(block_expert, n_used, xs, w_gu, w_down)


def _combine_kernel(s0_ref, s1_ref, h_ref, gate_ref, fg_ref, ys_ref, o_ref, buf, sem, *,
                    tm, final):
    base = pl.program_id(0) * tm

    def row_copy(r, k, src):
        return pltpu.make_async_copy(ys_ref.at[pl.ds(src, 1), :], buf.at[k, pl.ds(r, 1), :], sem)

    def issue(r, carry):
        row_copy(r, 0, s0_ref[base + r]).start()
        row_copy(r, 1, s1_ref[base + r]).start()
        return carry

    lax.fori_loop(0, tm, issue, 0, unroll=8)
    for k in range(2):
        pltpu.make_async_copy(ys_ref.at[pl.ds(0, tm), :], buf.at[k], sem).wait()
    gate = gate_ref[...]
    lo0, hi0 = _unpack_rows(buf[0])
    lo1, hi1 = _unpack_rows(buf[1])
    g0, g1 = gate[:, 0:1], gate[:, 1:2]
    y = h_ref[...] + jnp.concatenate([g0 * lo0 + g1 * lo1, g0 * hi0 + g1 * hi1], axis=1)
    if final:
        y = _rms(y, fg_ref[...])
    o_ref[...] = y


def _combine(h, gates, slot0, slot1, ys, final_gain, *, tm, final):
    t, d = h.shape
    return pl.pallas_call(
        functools.partial(_combine_kernel, tm=tm, final=final),
        out_shape=jax.ShapeDtypeStruct((t, d), F32),
        grid_spec=pltpu.PrefetchScalarGridSpec(
            num_scalar_prefetch=2,
            grid=(t // tm,),
            in_specs=[
                pl.BlockSpec((tm, d), lambda i, s0, s1: (i, 0)),
                pl.BlockSpec((tm, 2), lambda i, s0, s1: (i, 0)),
                pl.BlockSpec((1, d), lambda i, s0, s1: (0, 0)),
                pl.BlockSpec(memory_space=pl.ANY),
            ],
            out_specs=pl.BlockSpec((tm, d), lambda i, s0, s1: (i, 0)),
            scratch_shapes=[pltpu.VMEM((2, tm, d // 2), jnp.uint32), pltpu.SemaphoreType.DMA],
        ),
        compiler_params=_cparams(("arbitrary",)),
        name="moe_combine",
    )(slot0, slot1, h, gates, final_gain.reshape(1, d), ys)


def _block_table(counts, t):
    counts = counts.astype(jnp.int32)
    padded = (counts + MOE_BLOCK - 1) // MOE_BLOCK * MOE_BLOCK
    pend = jnp.cumsum(padded)
    pstart = (pend - padded).astype(jnp.int32)
    n_blocks = (2 * t + MOE_BLOCK - 1) // MOE_BLOCK + N_EXPERTS
    block_start = jnp.arange(n_blocks, dtype=jnp.int32) * MOE_BLOCK
    block_expert = jnp.minimum(jnp.sum((pend[None, :] <= block_start[:, None]).astype(jnp.int32), axis=1),
                               N_EXPERTS - 1).astype(jnp.int32)
    n_used = (pend[-1] // MOE_BLOCK).astype(jnp.int32).reshape(1)
    filled_to = (pstart + counts)[block_expert]
    partial_block = ((block_start + MOE_BLOCK > filled_to) | (block_start >= pend[-1])).astype(jnp.int32)
    return pstart, block_expert, n_used, partial_block, n_blocks * MOE_BLOCK


def _moe(h, gain, w_group, b_group, w_expert, b_expert, w_gu, w_down, layer, final_gain, *,
         final, tm):
    t, _ = h.shape
    pad = ROUTER_WIDTH - N_GROUPS - N_EXPERTS
    w_router = jnp.concatenate([w_group, w_expert, jnp.zeros((w_group.shape[0], pad), F32)], axis=1)
    b_router = jnp.concatenate([b_group, b_expert, jnp.zeros((pad,), F32)], axis=0)
    xn, code, gates, counts = _router(h, gain, w_router, b_router, tm=tm)
    pstart, block_expert, n_used, partial_block, cap = _block_table(counts[0], t)
    expert = lax.shift_right_logical(code, RANK_BITS)
    onehot = expert[:, :, None] == jnp.arange(N_EXPERTS, dtype=jnp.int32)
    slot = jnp.sum(jnp.where(onehot, pstart, 0), axis=-1) + (code & ((1 << RANK_BITS) - 1))
    slot0, slot1 = slot[:, 0], slot[:, 1]
    xs = _dispatch(xn, partial_block, slot0, slot1, cap, tm=tm)
    ys = _experts(xs, w_gu, w_down, layer, block_expert, n_used)
    return _combine(h, gates, slot0, slot1, ys, final_gain, tm=tm, final=final)


def _pick(n, pref):
    for c in pref:
        if n % c == 0:
            return c
    return n


def kernel(x, norm_mix, norm_ffn, gdn_w_in, gdn_conv_w, gdn_a_log, gdn_dt_bias, gdn_norm_w,
           gdn_w_out, sc_w_in, sc_conv_w, sc_w_out, moe_w_group, moe_b_group, moe_w_expert,
           moe_b_expert, moe_w_gu, moe_w_down, norm_final):
    batch, seq, d = x.shape
    t = batch * seq
    h = x.reshape(t, d)
    tm = _pick(t, (1024, 512, 256))
    tb = _pick(seq, (256, 128, 64))

    w_in = gdn_w_in[0]
    proj, ba = _norm_matmul_side(h, norm_mix[0], w_in[:, :MAIN_DIM].astype(BF16),
                                 w_in[:, MAIN_DIM:], tm=tm, tn=1024, out_dtype=F32)
    o = _gdn_core(proj, ba, gdn_conv_w[0], gdn_a_log[0], gdn_dt_bias[0], gdn_norm_w[0],
                  batch=batch, seq=seq, tb=tb, hp=GDN_HEADS_PER_STEP)
    h = _matmul_res(o, gdn_w_out[0].astype(BF16), h, tm=tm, tn=1024)
    h = _moe(h, norm_ffn[0], moe_w_group[0], moe_b_group[0], moe_w_expert[0], moe_b_expert[0],
             moe_w_gu, moe_w_down, 0, norm_final, final=False, tm=_pick(t, (256,)))

    y = _sconv_proj(h, norm_mix[1], sc_w_in[0].astype(BF16), sc_conv_w[0], seq=seq,
                    tm=_pick(seq, (512, 256)), tn=512)
    h = _matmul_res(y, sc_w_out[0].astype(BF16), h, tm=tm, tn=1024)
    h = _moe(h, norm_ffn[1], moe_w_group[1], moe_b_group[1], moe_w_expert[1], moe_b_expert[1],
             moe_w_gu, moe_w_down, 1, norm_final, final=True, tm=_pick(t, (256,)))
    return h.reshape(batch, seq, d)
```

```python
import functools

import jax
import jax.numpy as jnp
from jax import lax
from jax.experimental import pallas as pl
from jax.experimental.pallas import tpu as pltpu

EPS = 1e-6
F32 = jnp.float32
BF16 = jnp.bfloat16

QK_HEADS = 16
V_HEADS = 32
HEAD_DIM = 128
KEY_DIM = QK_HEADS * HEAD_DIM
VAL_DIM = V_HEADS * HEAD_DIM
QKV_DIM = 2 * KEY_DIM + VAL_DIM
MAIN_DIM = QKV_DIM + VAL_DIM
GDN_CONV = 4
CHUNK = 64
HALO = 8
GDN_HEADS_PER_STEP = 4
GATE_ROWS = 16

SC_WIDTH = 3

N_GROUPS = 8
EXPERTS_PER_GROUP = 8
N_EXPERTS = N_GROUPS * EXPERTS_PER_GROUP
D_EXPERT = 512
MOE_BLOCK = 512
ROUTER_WIDTH = 128
RANK_BITS = 16

VMEM_LIMIT = 56 * 1024 * 1024


def _cparams(sem):
    return pltpu.CompilerParams(dimension_semantics=sem, vmem_limit_bytes=VMEM_LIMIT)


def _rms(x, gain):
    return x * lax.rsqrt(jnp.mean(x * x, axis=-1, keepdims=True) + EPS) * gain


def _sigmoid(x):
    return 0.5 * jnp.tanh(0.5 * x) + 0.5


def _softplus(x):
    return jnp.maximum(x, 0.0) + jnp.log1p(jnp.exp(-jnp.abs(x)))


def _pack_rows(v):
    half = v.shape[1] // 2
    bits = pltpu.bitcast(v, jnp.uint32)
    rounded = bits + jnp.uint32(0x7FFF) + ((bits >> 16) & jnp.uint32(1))
    return (rounded[:, :half] >> 16) | (rounded[:, half:] & jnp.uint32(0xFFFF0000))


def _unpack_rows(p):
    lo = pltpu.bitcast(p << 16, F32)
    hi = pltpu.bitcast(p & jnp.uint32(0xFFFF0000), F32)
    return lo, hi


def _split2(w):
    hi = w.astype(BF16)
    lo = (w - hi.astype(F32)).astype(BF16)
    return jnp.concatenate([hi, lo], axis=-1)


def _dot_split(x, w2):
    n = w2.shape[-1] // 2
    x_hi = x.astype(BF16)
    x_lo = (x - x_hi.astype(F32)).astype(BF16)
    a = jnp.dot(x_hi, w2, preferred_element_type=F32)
    b = jnp.dot(x_lo, w2[:, :n], preferred_element_type=F32)
    return (a[:, :n] + a[:, n:]) + b


def _norm_matmul_side_kernel(x_ref, g_ref, w_ref, ws_ref, o_ref, os_ref, xn_ref):
    @pl.when(pl.program_id(1) == 0)
    def _():
        xn = _rms(x_ref[...], g_ref[...])
        xn_ref[...] = xn.astype(BF16)
        os_ref[...] = _dot_split(xn, ws_ref[...])

    o_ref[...] = jnp.dot(xn_ref[...], w_ref[...], preferred_element_type=F32).astype(o_ref.dtype)


def _norm_matmul_side(x, gain, w, w_side, *, tm, tn, out_dtype):
    t, k = x.shape
    n, ns = w.shape[1], w_side.shape[1]
    return pl.pallas_call(
        _norm_matmul_side_kernel,
        out_shape=(jax.ShapeDtypeStruct((t, n), out_dtype), jax.ShapeDtypeStruct((t, ns), F32)),
        grid=(t // tm, n // tn),
        in_specs=[
            pl.BlockSpec((tm, k), lambda i, j: (i, 0)),
            pl.BlockSpec((1, k), lambda i, j: (0, 0)),
            pl.BlockSpec((k, tn), lambda i, j: (0, j)),
            pl.BlockSpec((k, 2 * ns), lambda i, j: (0, 0)),
        ],
        out_specs=(pl.BlockSpec((tm, tn), lambda i, j: (i, j)),
                   pl.BlockSpec((tm, ns), lambda i, j: (i, 0))),
        scratch_shapes=[pltpu.VMEM((tm, k), BF16)],
        compiler_params=_cparams(("parallel", "arbitrary")),
        name="norm_matmul_side",
    )(x, gain.reshape(1, k), w, _split2(w_side))


def _matmul_res_kernel(a_ref, w_ref, r_ref, o_ref):
    o_ref[...] = r_ref[...] + jnp.dot(a_ref[...], w_ref[...], preferred_element_type=F32)


def _matmul_res(a, w, res, *, tm, tn):
    t, k = a.shape
    n = w.shape[1]
    return pl.pallas_call(
        _matmul_res_kernel,
        out_shape=jax.ShapeDtypeStruct((t, n), F32),
        grid=(t // tm, n // tn),
        in_specs=[
            pl.BlockSpec((tm, k), lambda i, j: (i, 0)),
            pl.BlockSpec((k, tn), lambda i, j: (0, j)),
            pl.BlockSpec((tm, tn), lambda i, j: (i, j)),
        ],
        out_specs=pl.BlockSpec((tm, tn), lambda i, j: (i, j)),
        compiler_params=_cparams(("parallel", "arbitrary")),
        name="matmul_res",
    )(a, w, res)


def _gdn_kernel(q_ref, k_ref, v_ref, z_ref, row_ref, cwq_ref, cwk_ref, cwv_ref,
                alog_ref, dtb_ref, nw_ref, o_ref, state_ref, qbuf, kbuf, vbuf, *, tb, hp):
    nchunk = tb // CHUNK

    @pl.when(pl.program_id(2) == 0)
    def _():
        state_ref[...] = jnp.zeros_like(state_ref)
        qbuf[0:HALO, :] = jnp.zeros((HALO, qbuf.shape[1]), F32)
        kbuf[0:HALO, :] = jnp.zeros((HALO, kbuf.shape[1]), F32)
        vbuf[0:HALO, :] = jnp.zeros((HALO, vbuf.shape[1]), F32)

    def conv_silu(x_ref, buf, cw_ref):
        buf[HALO:HALO + tb, :] = x_ref[...].astype(F32)
        xb = buf[...]
        acc = cw_ref[0:1, :] * xb
        for kk in range(1, GDN_CONV):
            acc = cw_ref[kk:kk + 1, :] * xb + pltpu.roll(acc, 1, axis=0)
        buf[0:HALO, :] = buf[tb:tb + HALO, :]
        acc = acc[HALO:, :]
        return acc * _sigmoid(acc)

    def l2norm(x):
        return x * lax.rsqrt(jnp.sum(x * x, axis=-1, keepdims=True) + EPS)

    q_all = conv_silu(q_ref, qbuf, cwq_ref)
    k_all = conv_silu(k_ref, kbuf, cwk_ref)
    v_all = conv_silu(v_ref, vbuf, cwv_ref)

    ri = lax.broadcasted_iota(jnp.int32, (tb, tb), 0)
    ci = lax.broadcasted_iota(jnp.int32, (tb, tb), 1)
    same_chunk = (ri // CHUNK) == (ci // CHUNK)
    cum_tot = jnp.concatenate([jnp.where(same_chunk & (ri <= ci), 1.0, 0.0),
                               jnp.where(same_chunk, 1.0, 0.0)], axis=1).astype(BF16)
    si = lax.broadcasted_iota(jnp.int32, (3 * GATE_ROWS, GATE_ROWS), 0)
    sj = lax.broadcasted_iota(jnp.int32, (3 * GATE_ROWS, GATE_ROWS), 1)
    fold3 = jnp.where(si % GATE_ROWS == sj, 1.0, 0.0).astype(BF16)

    def split3(x):
        hi = x.astype(BF16).astype(F32)
        r1 = x - hi
        mid = r1.astype(BF16).astype(F32)
        lo = r1 - mid
        return jnp.concatenate([hi, mid, lo], axis=0).astype(BF16)

    r64 = lax.broadcasted_iota(jnp.int32, (CHUNK, CHUNK), 0)
    c64 = lax.broadcasted_iota(jnp.int32, (CHUNK, CHUNK), 1)
    tril = c64 <= r64
    strict = c64 < r64
    eye = jnp.where(c64 == r64, 1.0, 0.0).astype(F32)

    def mm(a, b):
        return lax.dot_general(a, b.astype(BF16), (((1,), (0,)), ((), ())),
                               preferred_element_type=F32)

    def mm_nt(a, b):
        return lax.dot_general(a.astype(BF16), b.astype(BF16), (((1,), (1,)), ((), ())),
                               preferred_element_type=F32)

    def mm_tn(a, b):
        return lax.dot_general(a.astype(BF16), b.astype(BF16), (((0,), (0,)), ((), ())),
                               preferred_element_type=F32)

    chunks = range(nchunk)
    rows = [slice(c * CHUNK, (c + 1) * CHUNK) for c in chunks]
    pairs = [(p, c) for p in range(hp) for c in chunks]
    heads = [(p, c, j) for p in range(hp) for c in chunks for j in range(2)]

    qc, kc, gate_col, gc_row = {}, {}, {}, {}
    for p in range(hp):
        hl = slice(p * HEAD_DIM, (p + 1) * HEAD_DIM)
        q = l2norm(q_all[:, hl]) * (HEAD_DIM ** -0.5)
        k = l2norm(k_all[:, hl])
        for c in chunks:
            qc[p, c], kc[p, c] = q[rows[c]], k[rows[c]]
        alog, dtb = alog_ref[p].reshape(2, 1), dtb_ref[p].reshape(2, 1)
        beta_r = _sigmoid(row_ref[p, 0:2, :])
        g_r = -jnp.exp(alog) * _softplus(row_ref[p, 2:4, :] + dtb)
        r3 = jnp.dot(split3(g_r), cum_tot, preferred_element_type=F32)
        r3 = (r3[0:2, :] + r3[2:4, :]) + r3[4:6, :]
        gc_r, gtot_r = r3[:, :tb], r3[:, tb:]
        egc_r = jnp.exp(gc_r)
        gc_row[p] = gc_r
        gate_rows = jnp.concatenate(
            [beta_r, gc_r, egc_r, jnp.exp(gtot_r - gc_r), jnp.exp(gtot_r), beta_r * egc_r,
             jnp.zeros((GATE_ROWS - 12, tb), F32)], axis=0)
        gate_col[p] = lax.dot_general(split3(gate_rows), fold3, (((0,), (0,)), ((), ())),
                                      preferred_element_type=F32)

    kk_t = {pc: mm_nt(kc[pc], kc[pc]) for pc in pairs}
    qk_t = {pc: mm_nt(qc[pc], kc[pc]) for pc in pairs}

    beta, egc, kdec, etot, bege, lower, attn = {}, {}, {}, {}, {}, {}, {}
    for h in heads:
        p, c, j = h
        col = lambda i: gate_col[p][rows[c], i + j:i + j + 1]
        beta[h], gcc, egc[h], kdec[h], bege[h] = col(0), col(2), col(4), col(6), col(10)
        etot[h] = col(8)[0:1, :]
        gcr = gc_row[p][j:j + 1, rows[c]]
        decay = jnp.exp(jnp.where(tril, gcc - gcr, -jnp.inf))
        lower[h] = jnp.where(strict, beta[h] * kk_t[p, c] * decay, 0.0)
        attn[h] = jnp.where(tril, qk_t[p, c] * decay, 0.0)

    inv = {h: eye - lower[h] for h in heads}
    power = dict(lower)
    for _ in range(5):
        power = {h: mm(power[h], power[h]) for h in heads}
        inv = {h: inv[h] + mm(inv[h], power[h]) for h in heads}

    uw = {}
    for h in heads:
        p, c, j = h
        vl = slice((2 * p + j) * HEAD_DIM, (2 * p + j + 1) * HEAD_DIM)
        rhs = jnp.concatenate([v_all[rows[c], vl] * beta[h], kc[p, c] * bege[h]], axis=1)
        uw[h] = mm(inv[h], rhs)
    nk = {h: mm_tn(kc[h[0], h[1]] * kdec[h], uw[h]) for h in heads}
    ao = {h: mm(attn[h], uw[h]) for h in heads}
    qp = {h: (qc[h[0], h[1]] * egc[h] - ao[h][:, HEAD_DIM:]).astype(BF16) for h in heads}

    state = {(p, j): state_ref[2 * p + j] for p in range(hp) for j in range(2)}
    seen = {}
    for c in chunks:
        for p in range(hp):
            for j in range(2):
                h = (p, c, j)
                s_bf = state[p, j].astype(BF16)
                seen[h] = s_bf
                n_c, k_c = nk[h][:, :HEAD_DIM], nk[h][:, HEAD_DIM:]
                state[p, j] = state[p, j] * etot[h] + (n_c - mm(k_c, s_bf))
    for p in range(hp):
        for j in range(2):
            state_ref[2 * p + j] = state[p, j]

    nw = nw_ref[...]
    for h in heads:
        p, c, j = h
        vl = slice((2 * p + j) * HEAD_DIM, (2 * p + j + 1) * HEAD_DIM)
        o = jnp.dot(qp[h], seen[h], preferred_element_type=F32) + ao[h][:, :HEAD_DIM]
        zc = z_ref[rows[c], vl].astype(F32)
        o = o * lax.rsqrt(jnp.mean(o * o, axis=-1, keepdims=True) + EPS) * nw
        o_ref[rows[c], vl] = (o * (zc * _sigmoid(zc))).astype(o_ref.dtype)


def _gdn_core(proj, ba, conv_w, a_log, dt_bias, norm_w, *, batch, seq, tb, hp):
    proj3 = proj.reshape(batch, seq, MAIN_DIM)
    ba4 = ba.reshape(batch, seq, 2, QK_HEADS, 2)
    rows = jnp.transpose(ba4, (0, 3, 2, 4, 1)).reshape(batch, QK_HEADS, 4, seq)
    alog2 = a_log.reshape(QK_HEADS, 1, 2)
    dtb2 = dt_bias.reshape(QK_HEADS, 1, 2)
    qw, vw = hp * HEAD_DIM, 2 * hp * HEAD_DIM
    kq = KEY_DIM // qw
    vq = (2 * KEY_DIM) // vw
    zq = QKV_DIM // vw
    out = pl.pallas_call(
        functools.partial(_gdn_kernel, tb=tb, hp=hp),
        out_shape=jax.ShapeDtypeStruct((batch, seq, VAL_DIM), BF16),
        grid=(batch, QK_HEADS // hp, seq // tb),
        in_specs=[
            pl.BlockSpec((None, tb, qw), lambda b, h, t: (b, t, h)),
            pl.BlockSpec((None, tb, qw), lambda b, h, t: (b, t, kq + h)),
            pl.BlockSpec((None, tb, vw), lambda b, h, t: (b, t, vq + h)),
            pl.BlockSpec((None, tb, vw), lambda b, h, t: (b, t, zq + h)),
            pl.BlockSpec((None, hp, 4, tb), lambda b, h, t: (b, h, 0, t)),
            pl.BlockSpec((GDN_CONV, qw), lambda b, h, t: (0, h)),
            pl.BlockSpec((GDN_CONV, qw), lambda b, h, t: (0, kq + h)),
            pl.BlockSpec((GDN_CONV, vw), lambda b, h, t: (0, vq + h)),
            pl.BlockSpec((hp, 1, 2), lambda b, h, t: (h, 0, 0)),
            pl.BlockSpec((hp, 1, 2), lambda b, h, t: (h, 0, 0)),
            pl.BlockSpec((1, HEAD_DIM), lambda b, h, t: (0, 0)),
        ],
        out_specs=pl.BlockSpec((None, tb, vw), lambda b, h, t: (b, t, h)),
        scratch_shapes=[
            pltpu.VMEM((2 * hp, HEAD_DIM, HEAD_DIM), F32),
            pltpu.VMEM((tb + HALO, qw), F32),
            pltpu.VMEM((tb + HALO, qw), F32),
            pltpu.VMEM((tb + HALO, vw), F32),
        ],
        compiler_params=_cparams(("parallel", "parallel", "arbitrary")),
        name="gdn_core",
    )(proj3, proj3, proj3, proj3, rows, conv_w, conv_w, conv_w, alog2, dtb2,
      norm_w.reshape(1, HEAD_DIM))
    return out.reshape(batch * seq, VAL_DIM)


def _sconv_proj_kernel(x_ref, g_ref, wb_ref, wc_ref, wh_ref, cw_ref, o_ref, xn_ref, halo_ref, buf,
                       *, tm, tiles_per_seq):
    i, j = pl.program_id(0), pl.program_id(1)

    @pl.when(j == 0)
    def _():
        xn_ref[...] = _rms(x_ref[...], g_ref[...]).astype(BF16)

    @pl.when(i % tiles_per_seq == 0)
    def _():
        halo_ref[j] = jnp.zeros((HALO, halo_ref.shape[2]), F32)

    xn = xn_ref[...]
    gate_b = jnp.dot(xn, wb_ref[...], preferred_element_type=F32)
    u = (jnp.dot(xn, wc_ref[...], preferred_element_type=F32)
         * jnp.dot(xn, wh_ref[...], preferred_element_type=F32))
    buf[0:HALO, :] = halo_ref[j]
    buf[HALO:HALO + tm, :] = u
    halo_ref[j] = u[tm - HALO:tm, :]
    ub = buf[...]
    acc = cw_ref[0:1, :] * ub
    for kk in range(1, SC_WIDTH):
        acc = cw_ref[kk:kk + 1, :] * ub + pltpu.roll(acc, 1, axis=0)
    o_ref[...] = (gate_b * acc[HALO:, :]).astype(o_ref.dtype)


def _sconv_proj(x, gain, w_in, conv_w, *, seq, tm, tn):
    t, k = x.shape
    d = conv_w.shape[1]
    nj = d // tn
    return pl.pallas_call(
        functools.partial(_sconv_proj_kernel, tm=tm, tiles_per_seq=seq // tm),
        out_shape=jax.ShapeDtypeStruct((t, d), BF16),
        grid=(t // tm, nj),
        in_specs=[
            pl.BlockSpec((tm, k), lambda i, j: (i, 0)),
            pl.BlockSpec((1, k), lambda i, j: (0, 0)),
            pl.BlockSpec((k, tn), lambda i, j: (0, j)),
            pl.BlockSpec((k, tn), lambda i, j: (0, nj + j)),
            pl.BlockSpec((k, tn), lambda i, j: (0, 2 * nj + j)),
            pl.BlockSpec((SC_WIDTH, tn), lambda i, j: (0, j)),
        ],
        out_specs=pl.BlockSpec((tm, tn), lambda i, j: (i, j)),
        scratch_shapes=[pltpu.VMEM((tm, k), BF16), pltpu.VMEM((nj, HALO, tn), F32),
                        pltpu.VMEM((tm + HALO, tn), F32)],
        compiler_params=_cparams(("arbitrary", "arbitrary")),
        name="sconv_proj",
    )(x, gain.reshape(1, k), w_in, w_in, w_in, conv_w)


def _router_kernel(h_ref, g_ref, wr_ref, br_ref, xn_ref, code_ref, gate_ref, cnt_ref, carry_ref):
    @pl.when(pl.program_id(0) == 0)
    def _():
        carry_ref[...] = jnp.zeros_like(carry_ref)

    xn = _rms(h_ref[...], g_ref[...])
    xn_ref[...] = _pack_rows(xn)
    logits = _dot_split(xn, wr_ref[...]) + br_ref[...]
    tm = logits.shape[0]
    glog = logits[:, 0:N_GROUPS]
    elog = logits[:, N_GROUPS:N_GROUPS + N_EXPERTS]
    gl = lax.broadcasted_iota(jnp.int32, (tm, N_GROUPS), 1)
    gmax = jnp.max(glog, axis=-1, keepdims=True)
    group = jnp.min(jnp.where(glog == gmax, gl, N_GROUPS), axis=-1, keepdims=True)
    p_group = 1.0 / jnp.sum(jnp.exp(glog - gmax), axis=-1, keepdims=True)
    el = lax.broadcasted_iota(jnp.int32, (tm, N_EXPERTS), 1)
    neg = jnp.float32(-jnp.inf)
    within = jnp.where((el // EXPERTS_PER_GROUP) == group, elog, neg)
    m1 = jnp.max(within, axis=-1, keepdims=True)
    i1 = jnp.min(jnp.where(within == m1, el, N_EXPERTS), axis=-1, keepdims=True)
    rest = jnp.where(el == i1, neg, within)
    m2 = jnp.max(rest, axis=-1, keepdims=True)
    i2 = jnp.min(jnp.where(rest == m2, el, N_EXPERTS), axis=-1, keepdims=True)
    e2 = jnp.exp(m2 - m1)
    g1 = p_group / (1.0 + e2)
    g2 = p_group * e2 / (1.0 + e2)

    pick1, pick2 = el == i1, el == i2
    onehot = jnp.where(pick1 | pick2, 1.0, 0.0)
    rr = lax.broadcasted_iota(jnp.int32, (tm, tm), 0)
    cc = lax.broadcasted_iota(jnp.int32, (tm, tm), 1)
    before = jnp.where(cc < rr, 1.0, 0.0).astype(BF16)
    prefix = jnp.dot(before, onehot.astype(BF16), preferred_element_type=F32) + carry_ref[...]
    r1 = jnp.sum(jnp.where(pick1, prefix, 0.0), axis=-1, keepdims=True).astype(jnp.int32)
    r2 = jnp.sum(jnp.where(pick2, prefix, 0.0), axis=-1, keepdims=True).astype(jnp.int32)
    carry_ref[...] = carry_ref[...] + jnp.sum(onehot, axis=0, keepdims=True)
    cnt_ref[...] = carry_ref[...]

    lane2 = lax.broadcasted_iota(jnp.int32, (tm, 2), 1)
    code1 = jnp.left_shift(i1, RANK_BITS) | r1
    code2 = jnp.left_shift(i2, RANK_BITS) | r2
    code_ref[...] = jnp.where(lane2 == 0, code1, code2)
    gate_ref[...] = jnp.where(lane2 == 0, g1, g2)


def _router(h, gain, w_router, b_router, *, tm):
    t, d = h.shape
    nr = w_router.shape[1]
    return pl.pallas_call(
        _router_kernel,
        out_shape=(jax.ShapeDtypeStruct((t, d // 2), jnp.uint32),
                   jax.ShapeDtypeStruct((t, 2), jnp.int32),
                   jax.ShapeDtypeStruct((t, 2), F32),
                   jax.ShapeDtypeStruct((1, N_EXPERTS), F32)),
        grid=(t // tm,),
        in_specs=[
            pl.BlockSpec((tm, d), lambda i: (i, 0)),
            pl.BlockSpec((1, d), lambda i: (0, 0)),
            pl.BlockSpec((d, 2 * nr), lambda i: (0, 0)),
            pl.BlockSpec((1, nr), lambda i: (0, 0)),
        ],
        out_specs=(pl.BlockSpec((tm, d // 2), lambda i: (i, 0)),
                   pl.BlockSpec((tm, 2), lambda i: (i, 0)),
                   pl.BlockSpec((tm, 2), lambda i: (i, 0)),
                   pl.BlockSpec((1, N_EXPERTS), lambda i: (0, 0))),
        scratch_shapes=[pltpu.VMEM((1, N_EXPERTS), F32)],
        compiler_params=_cparams(("arbitrary",)),
        name="moe_router",
    )(h, gain.reshape(1, d), _split2(w_router), b_router.reshape(1, nr))


def _dispatch_kernel(pf_ref, s0_ref, s1_ref, x_ref, xs_ref, zbuf, sem, zsem, *, tm, nb):
    base = pl.program_id(0) * tm

    @pl.when(pl.program_id(0) == 0)
    def _():
        zbuf[...] = jnp.zeros_like(zbuf)

        def zero_copy(b):
            return pltpu.make_async_copy(zbuf, xs_ref.at[pl.ds(b * MOE_BLOCK, MOE_BLOCK), :], zsem)

        def zstart(b, carry):
            @pl.when(pf_ref[b] != 0)
            def _():
                zero_copy(b).start()
            return carry

        def zwait(b, carry):
            @pl.when(pf_ref[b] != 0)
            def _():
                zero_copy(b).wait()
            return carry

        lax.fori_loop(0, nb, zstart, 0)
        lax.fori_loop(0, nb, zwait, 0)

    def row_copy(r, dest):
        return pltpu.make_async_copy(x_ref.at[pl.ds(r, 1), :], xs_ref.at[pl.ds(dest, 1), :], sem)

    def issue(r, carry):
        row_copy(r, s0_ref[base + r]).start()
        row_copy(r, s1_ref[base + r]).start()
        return carry

    lax.fori_loop(0, tm, issue, 0, unroll=8)
    whole = pltpu.make_async_copy(x_ref, xs_ref.at[pl.ds(0, tm), :], sem)
    whole.wait()
    whole.wait()


def _dispatch(xn, partial_block, slot0, slot1, cap, *, tm):
    t, d = xn.shape
    return pl.pallas_call(
        functools.partial(_dispatch_kernel, tm=tm, nb=cap // MOE_BLOCK),
        out_shape=jax.ShapeDtypeStruct((cap, d), xn.dtype),
        grid_spec=pltpu.PrefetchScalarGridSpec(
            num_scalar_prefetch=3,
            grid=(t // tm,),
            in_specs=[pl.BlockSpec((tm, d), lambda i, pf, s0, s1: (i, 0))],
            out_specs=pl.BlockSpec(memory_space=pl.ANY),
            scratch_shapes=[pltpu.VMEM((MOE_BLOCK, d), xn.dtype), pltpu.SemaphoreType.DMA,
                            pltpu.SemaphoreType.DMA],
        ),
        compiler_params=_cparams(("arbitrary",)),
        name="moe_dispatch",
    )(partial_block, slot0, slot1, xn)


def _expert_kernel(be_ref, nu_ref, x_ref, wgu_ref, wd_ref, o_ref, wgu_bf, wd_bf):
    i = pl.program_id(0)
    used = i < nu_ref[0]
    new_expert = (i == 0) | (be_ref[i] != be_ref[jnp.maximum(i - 1, 0)])

    @pl.when(used & new_expert)
    def _():
        wgu_bf[...] = wgu_ref[...].astype(BF16)
        wd_bf[...] = wd_ref[...].astype(BF16)

    @pl.when(used)
    def _():
        x_lo, x_hi = _unpack_rows(x_ref[...])
        half = x_lo.shape[1]
        gu = (jnp.dot(x_lo.astype(BF16), wgu_bf[0:half, :], preferred_element_type=F32)
              + jnp.dot(x_hi.astype(BF16), wgu_bf[half:, :], preferred_element_type=F32))
        gt, up = gu[:, :D_EXPERT], gu[:, D_EXPERT:]
        act = (gt * _sigmoid(gt) * up).astype(BF16)
        o_ref[...] = _pack_rows(jnp.dot(act, wd_bf[...], preferred_element_type=F32))

    @pl.when(jnp.logical_not(used))
    def _():
        o_ref[...] = jnp.zeros_like(o_ref)


def _experts(xs, w_gu, w_down, layer, block_expert, n_used):
    cap, dp = xs.shape
    d = 2 * dp
    nb = cap // MOE_BLOCK

    def blk(i, be, nu):
        return jnp.maximum(jnp.minimum(i, nu[0] - 1), 0)

    return pl.pallas_call(
        _expert_kernel,
        out_shape=jax.ShapeDtypeStruct((cap, dp), jnp.uint32),
        grid_spec=pltpu.PrefetchScalarGridSpec(
            num_scalar_prefetch=2,
            grid=(nb,),
            in_specs=[
                pl.BlockSpec((MOE_BLOCK, dp), lambda i, be, nu: (blk(i, be, nu), 0)),
                pl.BlockSpec((None, None, d, 2 * D_EXPERT),
                             lambda i, be, nu: (layer, be[blk(i, be, nu)], 0, 0)),
                pl.BlockSpec((None, None, D_EXPERT, d),
                             lambda i, be, nu: (layer, be[blk(i, be, nu)], 0, 0)),
            ],
            out_specs=pl.BlockSpec((MOE_BLOCK, dp), lambda i, be, nu: (i, 0)),
            scratch_shapes=[pltpu.VMEM((d, 2 * D_EXPERT), BF16), pltpu.VMEM((D_EXPERT, d), BF16)],
        ),
        compiler_params=_cparams(("arbitrary",)),
        name="moe_experts",
    )(block_expert, n_used, xs, w_gu, w_down)


def _combine_kernel(s0_ref, s1_ref, h_ref, gate_ref, fg_ref, ys_ref, o_ref, buf, sem, *,
                    tm, final):
    base = pl.program_id(0) * tm

    def row_copy(r, k, src):
        return pltpu.make_async_copy(ys_ref.at[pl.ds(src, 1), :], buf.at[k, pl.ds(r, 1), :], sem)

    def issue(r, carry):
        row_copy(r, 0, s0_ref[base + r]).start()
        row_copy(r, 1, s1_ref[base + r]).start()
        return carry

    lax.fori_loop(0, tm, issue, 0, unroll=8)
    for k in range(2):
        pltpu.make_async_copy(ys_ref.at[pl.ds(0, tm), :], buf.at[k], sem).wait()
    gate = gate_ref[...]
    lo0, hi0 = _unpack_rows(buf[0])
    lo1, hi1 = _unpack_rows(buf[1])
    g0, g1 = gate[:, 0:1], gate[:, 1:2]
    y = h_ref[...] + jnp.concatenate([g0 * lo0 + g1 * lo1, g0 * hi0 + g1 * hi1], axis=1)
    if final:
        y = _rms(y, fg_ref[...])
    o_ref[...] = y


def _combine(h, gates, slot0, slot1, ys, final_gain, *, tm, final):
    t, d = h.shape
    return pl.pallas_call(
        functools.partial(_combine_kernel, tm=tm, final=final),
        out_shape=jax.ShapeDtypeStruct((t, d), F32),
        grid_spec=pltpu.PrefetchScalarGridSpec(
            num_scalar_prefetch=2,
            grid=(t // tm,),
            in_specs=[
                pl.BlockSpec((tm, d), lambda i, s0, s1: (i, 0)),
                pl.BlockSpec((tm, 2), lambda i, s0, s1: (i, 0)),
                pl.BlockSpec((1, d), lambda i, s0, s1: (0, 0)),
                pl.BlockSpec(memory_space=pl.ANY),
            ],
            out_specs=pl.BlockSpec((tm, d), lambda i, s0, s1: (i, 0)),
            scratch_shapes=[pltpu.VMEM((2, tm, d // 2), jnp.uint32), pltpu.SemaphoreType.DMA],
        ),
        compiler_params=_cparams(("arbitrary",)),
        name="moe_combine",
    )(slot0, slot1, h, gates, final_gain.reshape(1, d), ys)


def _block_table(counts, t):
    counts = counts.astype(jnp.int32)
    padded = (counts + MOE_BLOCK - 1) // MOE_BLOCK * MOE_BLOCK
    pend = jnp.cumsum(padded)
    pstart = (pend - padded).astype(jnp.int32)
    n_blocks = (2 * t + MOE_BLOCK - 1) // MOE_BLOCK + N_EXPERTS
    block_start = jnp.arange(n_blocks, dtype=jnp.int32) * MOE_BLOCK
    block_expert = jnp.minimum(jnp.sum((pend[None, :] <= block_start[:, None]).astype(jnp.int32), axis=1),
                               N_EXPERTS - 1).astype(jnp.int32)
    n_used = (pend[-1] // MOE_BLOCK).astype(jnp.int32).reshape(1)
    filled_to = (pstart + counts)[block_expert]
    partial_block = ((block_start + MOE_BLOCK > filled_to) | (block_start >= pend[-1])).astype(jnp.int32)
    return pstart, block_expert, n_used, partial_block, n_blocks * MOE_BLOCK


def _moe(h, gain, w_group, b_group, w_expert, b_expert, w_gu, w_down, layer, final_gain, *,
         final, tm):
    t, _ = h.shape
    pad = ROUTER_WIDTH - N_GROUPS - N_EXPERTS
    w_router = jnp.concatenate([w_group, w_expert, jnp.zeros((w_group.shape[0], pad), F32)], axis=1)
    b_router = jnp.concatenate([b_group, b_expert, jnp.zeros((pad,), F32)], axis=0)
    xn, code, gates, counts = _router(h, gain, w_router, b_router, tm=tm)
    pstart, block_expert, n_used, partial_block, cap = _block_table(counts[0], t)
    expert = lax.shift_right_logical(code, RANK_BITS)
    onehot = expert[:, :, None] == jnp.arange(N_EXPERTS, dtype=jnp.int32)
    slot = jnp.sum(jnp.where(onehot, pstart, 0), axis=-1) + (code & ((1 << RANK_BITS) - 1))
    slot0, slot1 = slot[:, 0], slot[:, 1]
    xs = _dispatch(xn, partial_block, slot0, slot1, cap, tm=tm)
    ys = _experts(xs, w_gu, w_down, layer, block_expert, n_used)
    return _combine(h, gates, slot0, slot1, ys, final_gain, tm=tm, final=final)


def _pick(n, pref):
    for c in pref:
        if n % c == 0:
            return c
    return n


def kernel(x, norm_mix, norm_ffn, gdn_w_in, gdn_conv_w, gdn_a_log, gdn_dt_bias, gdn_norm_w,
           gdn_w_out, sc_w_in, sc_conv_w, sc_w_out, moe_w_group, moe_b_group, moe_w_expert,
           moe_b_expert, moe_w_gu, moe_w_down, norm_final):
    batch, seq, d = x.shape
    t = batch * seq
    h = x.reshape(t, d)
    tm = _pick(t, (1024, 512, 256))
    tb = _pick(seq, (256, 128, 64))

    w_in = gdn_w_in[0]
    proj, ba = _norm_matmul_side(h, norm_mix[0], w_in[:, :MAIN_DIM].astype(BF16),
                                 w_in[:, MAIN_DIM:], tm=tm, tn=1024, out_dtype=F32)
    o = _gdn_core(proj, ba, gdn_conv_w[0], gdn_a_log[0], gdn_dt_bias[0], gdn_norm_w[0],
                  batch=batch, seq=seq, tb=tb, hp=GDN_HEADS_PER_STEP)
    h = _matmul_res(o, gdn_w_out[0].astype(BF16), h, tm=tm, tn=1024)
    h = _moe(h, norm_ffn[0], moe_w_group[0], moe_b_group[0], moe_w_expert[0], moe_b_expert[0],
             moe_w_gu, moe_w_down, 0, norm_final, final=False, tm=_pick(t, (512, 256)))

    y = _sconv_proj(h, norm_mix[1], sc_w_in[0].astype(BF16), sc_conv_w[0], seq=seq,
                    tm=_pick(seq, (1024, 512, 256)), tn=512)
    h = _matmul_res(y, sc_w_out[0].astype(BF16), h, tm=tm, tn=1024)
    h = _moe(h, norm_ffn[1], moe_w_group[1], moe_b_group[1], moe_w_expert[1], moe_b_expert[1],
             moe_w_gu, moe_w_down, 1, norm_final, final=True, tm=_pick(t, (512, 256)))
    return h.reshape(batch, seq, d)
```

```python
import functools

import jax
import jax.numpy as jnp
from jax import lax
from jax.experimental import pallas as pl
from jax.experimental.pallas import tpu as pltpu

EPS = 1e-6
F32 = jnp.float32
BF16 = jnp.bfloat16

QK_HEADS = 16
V_HEADS = 32
HEAD_DIM = 128
KEY_DIM = QK_HEADS * HEAD_DIM
VAL_DIM = V_HEADS * HEAD_DIM
QKV_DIM = 2 * KEY_DIM + VAL_DIM
MAIN_DIM = QKV_DIM + VAL_DIM
GDN_CONV = 4
CHUNK = 64
HALO = 8
GDN_HEADS_PER_STEP = 4
GATE_ROWS = 16

SC_WIDTH = 3

N_GROUPS = 8
EXPERTS_PER_GROUP = 8
N_EXPERTS = N_GROUPS * EXPERTS_PER_GROUP
D_EXPERT = 512
MOE_BLOCK = 512
ROUTER_WIDTH = 128
RANK_BITS = 16

VMEM_LIMIT = 56 * 1024 * 1024


def _cparams(sem):
    return pltpu.CompilerParams(dimension_semantics=sem, vmem_limit_bytes=VMEM_LIMIT)


def _rms(x, gain):
    return x * lax.rsqrt(jnp.mean(x * x, axis=-1, keepdims=True) + EPS) * gain


def _sigmoid(x):
    return 0.5 * jnp.tanh(0.5 * x) + 0.5


def _softplus(x):
    return jnp.maximum(x, 0.0) + jnp.log1p(jnp.exp(-jnp.abs(x)))


def _pack_rows(v):
    half = v.shape[1] // 2
    bits = pltpu.bitcast(v, jnp.uint32)
    rounded = bits + jnp.uint32(0x7FFF) + ((bits >> 16) & jnp.uint32(1))
    return (rounded[:, :half] >> 16) | (rounded[:, half:] & jnp.uint32(0xFFFF0000))


def _unpack_rows(p):
    lo = pltpu.bitcast(p << 16, F32)
    hi = pltpu.bitcast(p & jnp.uint32(0xFFFF0000), F32)
    return lo, hi


def _split2(w):
    hi = w.astype(BF16)
    lo = (w - hi.astype(F32)).astype(BF16)
    return jnp.concatenate([hi, lo], axis=-1)


def _dot_split(x, w2):
    n = w2.shape[-1] // 2
    x_hi = x.astype(BF16)
    x_lo = (x - x_hi.astype(F32)).astype(BF16)
    a = jnp.dot(x_hi, w2, preferred_element_type=F32)
    b = jnp.dot(x_lo, w2[:, :n], preferred_element_type=F32)
    return (a[:, :n] + a[:, n:]) + b


def _norm_matmul_side_kernel(x_ref, g_ref, w_ref, ws_ref, o_ref, os_ref, xn_ref):
    @pl.when(pl.program_id(1) == 0)
    def _():
        xn = _rms(x_ref[...], g_ref[...])
        xn_ref[...] = xn.astype(BF16)
        os_ref[...] = _dot_split(xn, ws_ref[...])

    o_ref[...] = jnp.dot(xn_ref[...], w_ref[...], preferred_element_type=F32).astype(o_ref.dtype)


def _norm_matmul_side(x, gain, w, w_side, *, tm, tn, out_dtype):
    t, k = x.shape
    n, ns = w.shape[1], w_side.shape[1]
    return pl.pallas_call(
        _norm_matmul_side_kernel,
        out_shape=(jax.ShapeDtypeStruct((t, n), out_dtype), jax.ShapeDtypeStruct((t, ns), F32)),
        grid=(t // tm, n // tn),
        in_specs=[
            pl.BlockSpec((tm, k), lambda i, j: (i, 0)),
            pl.BlockSpec((1, k), lambda i, j: (0, 0)),
            pl.BlockSpec((k, tn), lambda i, j: (0, j)),
            pl.BlockSpec((k, 2 * ns), lambda i, j: (0, 0)),
        ],
        out_specs=(pl.BlockSpec((tm, tn), lambda i, j: (i, j)),
                   pl.BlockSpec((tm, ns), lambda i, j: (i, 0))),
        scratch_shapes=[pltpu.VMEM((tm, k), BF16)],
        compiler_params=_cparams(("parallel", "arbitrary")),
        name="norm_matmul_side",
    )(x, gain.reshape(1, k), w, _split2(w_side))


def _matmul_res_kernel(a_ref, w_ref, r_ref, o_ref):
    o_ref[...] = r_ref[...] + jnp.dot(a_ref[...], w_ref[...], preferred_element_type=F32)


def _matmul_res(a, w, res, *, tm, tn):
    t, k = a.shape
    n = w.shape[1]
    return pl.pallas_call(
        _matmul_res_kernel,
        out_shape=jax.ShapeDtypeStruct((t, n), F32),
        grid=(t // tm, n // tn),
        in_specs=[
            pl.BlockSpec((tm, k), lambda i, j: (i, 0)),
            pl.BlockSpec((k, tn), lambda i, j: (0, j)),
            pl.BlockSpec((tm, tn), lambda i, j: (i, j)),
        ],
        out_specs=pl.BlockSpec((tm, tn), lambda i, j: (i, j)),
        compiler_params=_cparams(("parallel", "arbitrary")),
        name="matmul_res",
    )(a, w, res)


def _gdn_kernel(q_ref, k_ref, v_ref, z_ref, row_ref, cwq_ref, cwk_ref, cwv_ref,
                alog_ref, dtb_ref, nw_ref, o_ref, state_ref, qbuf, kbuf, vbuf, *, tb, hp):
    nchunk = tb // CHUNK

    @pl.when(pl.program_id(2) == 0)
    def _():
        state_ref[...] = jnp.zeros_like(state_ref)
        qbuf[0:HALO, :] = jnp.zeros((HALO, qbuf.shape[1]), F32)
        kbuf[0:HALO, :] = jnp.zeros((HALO, kbuf.shape[1]), F32)
        vbuf[0:HALO, :] = jnp.zeros((HALO, vbuf.shape[1]), F32)

    def conv_silu(x_ref, buf, cw_ref):
        buf[HALO:HALO + tb, :] = x_ref[...].astype(F32)
        xb = buf[...]
        acc = cw_ref[0:1, :] * xb
        for kk in range(1, GDN_CONV):
            acc = cw_ref[kk:kk + 1, :] * xb + pltpu.roll(acc, 1, axis=0)
        buf[0:HALO, :] = buf[tb:tb + HALO, :]
        acc = acc[HALO:, :]
        return acc * _sigmoid(acc)

    def l2norm(x):
        return x * lax.rsqrt(jnp.sum(x * x, axis=-1, keepdims=True) + EPS)

    q_all = conv_silu(q_ref, qbuf, cwq_ref)
    k_all = conv_silu(k_ref, kbuf, cwk_ref)
    v_all = conv_silu(v_ref, vbuf, cwv_ref)

    ri = lax.broadcasted_iota(jnp.int32, (tb, tb), 0)
    ci = lax.broadcasted_iota(jnp.int32, (tb, tb), 1)
    same_chunk = (ri // CHUNK) == (ci // CHUNK)
    cum_tot = jnp.concatenate([jnp.where(same_chunk & (ri <= ci), 1.0, 0.0),
                               jnp.where(same_chunk, 1.0, 0.0)], axis=1).astype(BF16)
    si = lax.broadcasted_iota(jnp.int32, (3 * GATE_ROWS, GATE_ROWS), 0)
    sj = lax.broadcasted_iota(jnp.int32, (3 * GATE_ROWS, GATE_ROWS), 1)
    fold3 = jnp.where(si % GATE_ROWS == sj, 1.0, 0.0).astype(BF16)

    def split3(x):
        hi = x.astype(BF16).astype(F32)
        r1 = x - hi
        mid = r1.astype(BF16).astype(F32)
        lo = r1 - mid
        return jnp.concatenate([hi, mid, lo], axis=0).astype(BF16)

    r64 = lax.broadcasted_iota(jnp.int32, (CHUNK, CHUNK), 0)
    c64 = lax.broadcasted_iota(jnp.int32, (CHUNK, CHUNK), 1)
    tril = c64 <= r64
    strict = c64 < r64
    eye = jnp.where(c64 == r64, 1.0, 0.0).astype(F32)

    def mm(a, b):
        return lax.dot_general(a, b.astype(BF16), (((1,), (0,)), ((), ())),
                               preferred_element_type=F32)

    def mm_nt(a, b):
        return lax.dot_general(a.astype(BF16), b.astype(BF16), (((1,), (1,)), ((), ())),
                               preferred_element_type=F32)

    def mm_tn(a, b):
        return lax.dot_general(a.astype(BF16), b.astype(BF16), (((0,), (0,)), ((), ())),
                               preferred_element_type=F32)

    chunks = range(nchunk)
    rows = [slice(c * CHUNK, (c + 1) * CHUNK) for c in chunks]
    pairs = [(p, c) for p in range(hp) for c in chunks]
    heads = [(p, c, j) for p in range(hp) for c in chunks for j in range(2)]

    qc, kc, gate_col, gc_row = {}, {}, {}, {}
    for p in range(hp):
        hl = slice(p * HEAD_DIM, (p + 1) * HEAD_DIM)
        q = l2norm(q_all[:, hl]) * (HEAD_DIM ** -0.5)
        k = l2norm(k_all[:, hl])
        for c in chunks:
            qc[p, c], kc[p, c] = q[rows[c]], k[rows[c]]
        alog, dtb = alog_ref[p].reshape(2, 1), dtb_ref[p].reshape(2, 1)
        beta_r = _sigmoid(row_ref[p, 0:2, :])
        g_r = -jnp.exp(alog) * _softplus(row_ref[p, 2:4, :] + dtb)
        r3 = jnp.dot(split3(g_r), cum_tot, preferred_element_type=F32)
        r3 = (r3[0:2, :] + r3[2:4, :]) + r3[4:6, :]
        gc_r, gtot_r = r3[:, :tb], r3[:, tb:]
        egc_r = jnp.exp(gc_r)
        gc_row[p] = gc_r
        gate_rows = jnp.concatenate(
            [beta_r, gc_r, egc_r, jnp.exp(gtot_r - gc_r), jnp.exp(gtot_r), beta_r * egc_r,
             jnp.zeros((GATE_ROWS - 12, tb), F32)], axis=0)
        gate_col[p] = lax.dot_general(split3(gate_rows), fold3, (((0,), (0,)), ((), ())),
                                      preferred_element_type=F32)

    kk_t = {pc: mm_nt(kc[pc], kc[pc]) for pc in pairs}
    qk_t = {pc: mm_nt(qc[pc], kc[pc]) for pc in pairs}

    beta, egc, kdec, etot, bege, lower, attn = {}, {}, {}, {}, {}, {}, {}
    for h in heads:
        p, c, j = h
        col = lambda i: gate_col[p][rows[c], i + j:i + j + 1]
        beta[h], gcc, egc[h], kdec[h], bege[h] = col(0), col(2), col(4), col(6), col(10)
        etot[h] = col(8)[0:1, :]
        gcr = gc_row[p][j:j + 1, rows[c]]
        decay = jnp.exp(jnp.where(tril, gcc - gcr, -jnp.inf))
        lower[h] = jnp.where(strict, beta[h] * kk_t[p, c] * decay, 0.0)
        attn[h] = jnp.where(tril, qk_t[p, c] * decay, 0.0)

    inv = {h: eye - lower[h] for h in heads}
    power = dict(lower)
    for _ in range(5):
        power = {h: mm(power[h], power[h]) for h in heads}
        inv = {h: inv[h] + mm(inv[h], power[h]) for h in heads}

    uw = {}
    for h in heads:
        p, c, j = h
        vl = slice((2 * p + j) * HEAD_DIM, (2 * p + j + 1) * HEAD_DIM)
        rhs = jnp.concatenate([v_all[rows[c], vl] * beta[h], kc[p, c] * bege[h]], axis=1)
        uw[h] = mm(inv[h], rhs)
    nk = {h: mm_tn(kc[h[0], h[1]] * kdec[h], uw[h]) for h in heads}
    ao = {h: mm(attn[h], uw[h]) for h in heads}
    qp = {h: (qc[h[0], h[1]] * egc[h] - ao[h][:, HEAD_DIM:]).astype(BF16) for h in heads}

    state = {(p, j): state_ref[2 * p + j] for p in range(hp) for j in range(2)}
    seen = {}
    for c in chunks:
        for p in range(hp):
            for j in range(2):
                h = (p, c, j)
                s_bf = state[p, j].astype(BF16)
                seen[h] = s_bf
                n_c, k_c = nk[h][:, :HEAD_DIM], nk[h][:, HEAD_DIM:]
                state[p, j] = state[p, j] * etot[h] + (n_c - mm(k_c, s_bf))
    for p in range(hp):
        for j in range(2):
            state_ref[2 * p + j] = state[p, j]

    nw = nw_ref[...]
    for h in heads:
        p, c, j = h
        vl = slice((2 * p + j) * HEAD_DIM, (2 * p + j + 1) * HEAD_DIM)
        o = jnp.dot(qp[h], seen[h], preferred_element_type=F32) + ao[h][:, :HEAD_DIM]
        zc = z_ref[rows[c], vl].astype(F32)
        o = o * lax.rsqrt(jnp.mean(o * o, axis=-1, keepdims=True) + EPS) * nw
        o_ref[rows[c], vl] = (o * (zc * _sigmoid(zc))).astype(o_ref.dtype)


def _gdn_core(proj, ba, conv_w, a_log, dt_bias, norm_w, *, batch, seq, tb, hp):
    proj3 = proj.reshape(batch, seq, MAIN_DIM)
    ba4 = ba.reshape(batch, seq, 2, QK_HEADS, 2)
    rows = jnp.transpose(ba4, (0, 3, 2, 4, 1)).reshape(batch, QK_HEADS, 4, seq)
    alog2 = a_log.reshape(QK_HEADS, 1, 2)
    dtb2 = dt_bias.reshape(QK_HEADS, 1, 2)
    qw, vw = hp * HEAD_DIM, 2 * hp * HEAD_DIM
    kq = KEY_DIM // qw
    vq = (2 * KEY_DIM) // vw
    zq = QKV_DIM // vw
    out = pl.pallas_call(
        functools.partial(_gdn_kernel, tb=tb, hp=hp),
        out_shape=jax.ShapeDtypeStruct((batch, seq, VAL_DIM), BF16),
        grid=(batch, QK_HEADS // hp, seq // tb),
        in_specs=[
            pl.BlockSpec((None, tb, qw), lambda b, h, t: (b, t, h)),
            pl.BlockSpec((None, tb, qw), lambda b, h, t: (b, t, kq + h)),
            pl.BlockSpec((None, tb, vw), lambda b, h, t: (b, t, vq + h)),
            pl.BlockSpec((None, tb, vw), lambda b, h, t: (b, t, zq + h)),
            pl.BlockSpec((None, hp, 4, tb), lambda b, h, t: (b, h, 0, t)),
            pl.BlockSpec((GDN_CONV, qw), lambda b, h, t: (0, h)),
            pl.BlockSpec((GDN_CONV, qw), lambda b, h, t: (0, kq + h)),
            pl.BlockSpec((GDN_CONV, vw), lambda b, h, t: (0, vq + h)),
            pl.BlockSpec((hp, 1, 2), lambda b, h, t: (h, 0, 0)),
            pl.BlockSpec((hp, 1, 2), lambda b, h, t: (h, 0, 0)),
            pl.BlockSpec((1, HEAD_DIM), lambda b, h, t: (0, 0)),
        ],
        out_specs=pl.BlockSpec((None, tb, vw), lambda b, h, t: (b, t, h)),
        scratch_shapes=[
            pltpu.VMEM((2 * hp, HEAD_DIM, HEAD_DIM), F32),
            pltpu.VMEM((tb + HALO, qw), F32),
            pltpu.VMEM((tb + HALO, qw), F32),
            pltpu.VMEM((tb + HALO, vw), F32),
        ],
        compiler_params=_cparams(("parallel", "parallel", "arbitrary")),
        name="gdn_core",
    )(proj3, proj3, proj3, proj3, rows, conv_w, conv_w, conv_w, alog2, dtb2,
      norm_w.reshape(1, HEAD_DIM))
    return out.reshape(batch * seq, VAL_DIM)


def _sconv_proj_kernel(x_ref, g_ref, wb_ref, wc_ref, wh_ref, cw_ref, o_ref, xn_ref, halo_ref, buf,
                       *, tm, tiles_per_seq):
    i, j = pl.program_id(0), pl.program_id(1)

    @pl.when(j == 0)
    def _():
        xn_ref[...] = _rms(x_ref[...], g_ref[...]).astype(BF16)

    @pl.when(i % tiles_per_seq == 0)
    def _():
        halo_ref[j] = jnp.zeros((HALO, halo_ref.shape[2]), F32)

    xn = xn_ref[...]
    gate_b = jnp.dot(xn, wb_ref[...], preferred_element_type=F32)
    u = (jnp.dot(xn, wc_ref[...], preferred_element_type=F32)
         * jnp.dot(xn, wh_ref[...], preferred_element_type=F32))
    buf[0:HALO, :] = halo_ref[j]
    buf[HALO:HALO + tm, :] = u
    halo_ref[j] = u[tm - HALO:tm, :]
    ub = buf[...]
    acc = cw_ref[0:1, :] * ub
    for kk in range(1, SC_WIDTH):
        acc = cw_ref[kk:kk + 1, :] * ub + pltpu.roll(acc, 1, axis=0)
    o_ref[...] = (gate_b * acc[HALO:, :]).astype(o_ref.dtype)


def _sconv_proj(x, gain, w_in, conv_w, *, seq, tm, tn):
    t, k = x.shape
    d = conv_w.shape[1]
    nj = d // tn
    return pl.pallas_call(
        functools.partial(_sconv_proj_kernel, tm=tm, tiles_per_seq=seq // tm),
        out_shape=jax.ShapeDtypeStruct((t, d), BF16),
        grid=(t // tm, nj),
        in_specs=[
            pl.BlockSpec((tm, k), lambda i, j: (i, 0)),
            pl.BlockSpec((1, k), lambda i, j: (0, 0)),
            pl.BlockSpec((k, tn), lambda i, j: (0, j)),
            pl.BlockSpec((k, tn), lambda i, j: (0, nj + j)),
            pl.BlockSpec((k, tn), lambda i, j: (0, 2 * nj + j)),
            pl.BlockSpec((SC_WIDTH, tn), lambda i, j: (0, j)),
        ],
        out_specs=pl.BlockSpec((tm, tn), lambda i, j: (i, j)),
        scratch_shapes=[pltpu.VMEM((tm, k), BF16), pltpu.VMEM((nj, HALO, tn), F32),
                        pltpu.VMEM((tm + HALO, tn), F32)],
        compiler_params=_cparams(("arbitrary", "arbitrary")),
        name="sconv_proj",
    )(x, gain.reshape(1, k), w_in, w_in, w_in, conv_w)


def _router_kernel(h_ref, g_ref, wr_ref, br_ref, xn_ref, code_ref, gate_ref, cnt_ref, carry_ref):
    @pl.when(pl.program_id(0) == 0)
    def _():
        carry_ref[...] = jnp.zeros_like(carry_ref)

    xn = _rms(h_ref[...], g_ref[...])
    xn_ref[...] = _pack_rows(xn)
    logits = _dot_split(xn, wr_ref[...]) + br_ref[...]
    tm = logits.shape[0]
    glog = logits[:, 0:N_GROUPS]
    elog = logits[:, N_GROUPS:N_GROUPS + N_EXPERTS]
    gl = lax.broadcasted_iota(jnp.int32, (tm, N_GROUPS), 1)
    gmax = jnp.max(glog, axis=-1, keepdims=True)
    group = jnp.min(jnp.where(glog == gmax, gl, N_GROUPS), axis=-1, keepdims=True)
    p_group = 1.0 / jnp.sum(jnp.exp(glog - gmax), axis=-1, keepdims=True)
    el = lax.broadcasted_iota(jnp.int32, (tm, N_EXPERTS), 1)
    neg = jnp.float32(-jnp.inf)
    within = jnp.where((el // EXPERTS_PER_GROUP) == group, elog, neg)
    m1 = jnp.max(within, axis=-1, keepdims=True)
    i1 = jnp.min(jnp.where(within == m1, el, N_EXPERTS), axis=-1, keepdims=True)
    rest = jnp.where(el == i1, neg, within)
    m2 = jnp.max(rest, axis=-1, keepdims=True)
    i2 = jnp.min(jnp.where(rest == m2, el, N_EXPERTS), axis=-1, keepdims=True)
    e2 = jnp.exp(m2 - m1)
    g1 = p_group / (1.0 + e2)
    g2 = p_group * e2 / (1.0 + e2)

    pick1, pick2 = el == i1, el == i2
    onehot = jnp.where(pick1 | pick2, 1.0, 0.0)
    rr = lax.broadcasted_iota(jnp.int32, (tm, tm), 0)
    cc = lax.broadcasted_iota(jnp.int32, (tm, tm), 1)
    before = jnp.where(cc < rr, 1.0, 0.0).astype(BF16)
    prefix = jnp.dot(before, onehot.astype(BF16), preferred_element_type=F32) + carry_ref[...]
    r1 = jnp.sum(jnp.where(pick1, prefix, 0.0), axis=-1, keepdims=True).astype(jnp.int32)
    r2 = jnp.sum(jnp.where(pick2, prefix, 0.0), axis=-1, keepdims=True).astype(jnp.int32)
    carry_ref[...] = carry_ref[...] + jnp.sum(onehot, axis=0, keepdims=True)
    cnt_ref[...] = carry_ref[...]

    lane2 = lax.broadcasted_iota(jnp.int32, (tm, 2), 1)
    code1 = jnp.left_shift(i1, RANK_BITS) | r1
    code2 = jnp.left_shift(i2, RANK_BITS) | r2
    code_ref[...] = jnp.where(lane2 == 0, code1, code2)
    gate_ref[...] = jnp.where(lane2 == 0, g1, g2)


def _router(h, gain, w_router, b_router, *, tm):
    t, d = h.shape
    nr = w_router.shape[1]
    return pl.pallas_call(
        _router_kernel,
        out_shape=(jax.ShapeDtypeStruct((t, d // 2), jnp.uint32),
                   jax.ShapeDtypeStruct((t, 2), jnp.int32),
                   jax.ShapeDtypeStruct((t, 2), F32),
                   jax.ShapeDtypeStruct((1, N_EXPERTS), F32)),
        grid=(t // tm,),
        in_specs=[
            pl.BlockSpec((tm, d), lambda i: (i, 0)),
            pl.BlockSpec((1, d), lambda i: (0, 0)),
            pl.BlockSpec((d, 2 * nr), lambda i: (0, 0)),
            pl.BlockSpec((1, nr), lambda i: (0, 0)),
        ],
        out_specs=(pl.BlockSpec((tm, d // 2), lambda i: (i, 0)),
                   pl.BlockSpec((tm, 2), lambda i: (i, 0)),
                   pl.BlockSpec((tm, 2), lambda i: (i, 0)),
                   pl.BlockSpec((1, N_EXPERTS), lambda i: (0, 0))),
        scratch_shapes=[pltpu.VMEM((1, N_EXPERTS), F32)],
        compiler_params=_cparams(("arbitrary",)),
        name="moe_router",
    )(h, gain.reshape(1, d), _split2(w_router), b_router.reshape(1, nr))


def _dispatch_kernel(pf_ref, s0_ref, s1_ref, x_ref, xs_ref, zbuf, sem, zsem, *, tm, nb):
    base = pl.program_id(0) * tm

    @pl.when(pl.program_id(0) == 0)
    def _():
        zbuf[...] = jnp.zeros_like(zbuf)

        def zero_copy(b):
            return pltpu.make_async_copy(zbuf, xs_ref.at[pl.ds(b * MOE_BLOCK, MOE_BLOCK), :], zsem)

        def zstart(b, carry):
            @pl.when(pf_ref[b] != 0)
            def _():
                zero_copy(b).start()
            return carry

        def zwait(b, carry):
            @pl.when(pf_ref[b] != 0)
            def _():
                zero_copy(b).wait()
            return carry

        lax.fori_loop(0, nb, zstart, 0)
        lax.fori_loop(0, nb, zwait, 0)

    def row_copy(r, dest):
        return pltpu.make_async_copy(x_ref.at[pl.ds(r, 1), :], xs_ref.at[pl.ds(dest, 1), :], sem)

    def issue(r, carry):
        row_copy(r, s0_ref[base + r]).start()
        row_copy(r, s1_ref[base + r]).start()
        return carry

    lax.fori_loop(0, tm, issue, 0, unroll=8)
    whole = pltpu.make_async_copy(x_ref, xs_ref.at[pl.ds(0, tm), :], sem)
    whole.wait()
    whole.wait()


def _dispatch(xn, partial_block, slot0, slot1, cap, *, tm):
    t, d = xn.shape
    return pl.pallas_call(
        functools.partial(_dispatch_kernel, tm=tm, nb=cap // MOE_BLOCK),
        out_shape=jax.ShapeDtypeStruct((cap, d), xn.dtype),
        grid_spec=pltpu.PrefetchScalarGridSpec(
            num_scalar_prefetch=3,
            grid=(t // tm,),
            in_specs=[pl.BlockSpec((tm, d), lambda i, pf, s0, s1: (i, 0))],
            out_specs=pl.BlockSpec(memory_space=pl.ANY),
            scratch_shapes=[pltpu.VMEM((MOE_BLOCK, d), xn.dtype), pltpu.SemaphoreType.DMA,
                            pltpu.SemaphoreType.DMA],
        ),
        compiler_params=_cparams(("arbitrary",)),
        name="moe_dispatch",
    )(partial_block, slot0, slot1, xn)


def _expert_kernel(be_ref, nu_ref, x_ref, wgu_ref, wd_ref, o_ref, wgu_bf, wd_bf):
    i = pl.program_id(0)
    used = i < nu_ref[0]
    new_expert = (i == 0) | (be_ref[i] != be_ref[jnp.maximum(i - 1, 0)])

    @pl.when(used & new_expert)
    def _():
        wgu_bf[...] = wgu_ref[...].astype(BF16)
        wd_bf[...] = wd_ref[...].astype(BF16)

    @pl.when(used)
    def _():
        x_lo, x_hi = _unpack_rows(x_ref[...])
        half = x_lo.shape[1]
        gu = (jnp.dot(x_lo.astype(BF16), wgu_bf[0:half, :], preferred_element_type=F32)
              + jnp.dot(x_hi.astype(BF16), wgu_bf[half:, :], preferred_element_type=F32))
        gt, up = gu[:, :D_EXPERT], gu[:, D_EXPERT:]
        act = (gt * _sigmoid(gt) * up).astype(BF16)
        o_ref[...] = _pack_rows(jnp.dot(act, wd_bf[...], preferred_element_type=F32))

    @pl.when(jnp.logical_not(used))
    def _():
        o_ref[...] = jnp.zeros_like(o_ref)


def _experts(xs, w_gu, w_down, layer, block_expert, n_used):
    cap, dp = xs.shape
    d = 2 * dp
    nb = cap // MOE_BLOCK

    def blk(i, be, nu):
        return jnp.maximum(jnp.minimum(i, nu[0] - 1), 0)

    return pl.pallas_call(
        _expert_kernel,
        out_shape=jax.ShapeDtypeStruct((cap, dp), jnp.uint32),
        grid_spec=pltpu.PrefetchScalarGridSpec(
            num_scalar_prefetch=2,
            grid=(nb,),
            in_specs=[
                pl.BlockSpec((MOE_BLOCK, dp), lambda i, be, nu: (blk(i, be, nu), 0)),
                pl.BlockSpec((None, None, d, 2 * D_EXPERT),
                             lambda i, be, nu: (layer, be[blk(i, be, nu)], 0, 0)),
                pl.BlockSpec((None, None, D_EXPERT, d),
                             lambda i, be, nu: (layer, be[blk(i, be, nu)], 0, 0)),
            ],
            out_specs=pl.BlockSpec((MOE_BLOCK, dp), lambda i, be, nu: (i, 0)),
            scratch_shapes=[pltpu.VMEM((d, 2 * D_EXPERT), BF16), pltpu.VMEM((D_EXPERT, d), BF16)],
        ),
        compiler_params=_cparams(("arbitrary",)),
        name="moe_experts",
    )(block_expert, n_used, xs, w_gu, w_down)


def _combine_kernel(s0_ref, s1_ref, h_ref, gate_ref, fg_ref, ys_ref, o_ref, buf, sem, *,
                    tm, final):
    base = pl.program_id(0) * tm

    def row_copy(r, k, src):
        return pltpu.make_async_copy(ys_ref.at[pl.ds(src, 1), :], buf.at[k, pl.ds(r, 1), :], sem)

    def issue(r, carry):
        row_copy(r, 0, s0_ref[base + r]).start()
        row_copy(r, 1, s1_ref[base + r]).start()
        return carry

    lax.fori_loop(0, tm, issue, 0, unroll=8)
    for k in range(2):
        pltpu.make_async_copy(ys_ref.at[pl.ds(0, tm), :], buf.at[k], sem).wait()
    gate = gate_ref[...]
    lo0, hi0 = _unpack_rows(buf[0])
    lo1, hi1 = _unpack_rows(buf[1])
    g0, g1 = gate[:, 0:1], gate[:, 1:2]
    y = h_ref[...] + jnp.concatenate([g0 * lo0 + g1 * lo1, g0 * hi0 + g1 * hi1], axis=1)
    if final:
        y = _rms(y, fg_ref[...])
    o_ref[...] = y


def _combine(h, gates, slot0, slot1, ys, final_gain, *, tm, final):
    t, d = h.shape
    return pl.pallas_call(
        functools.partial(_combine_kernel, tm=tm, final=final),
        out_shape=jax.ShapeDtypeStruct((t, d), F32),
        grid_spec=pltpu.PrefetchScalarGridSpec(
            num_scalar_prefetch=2,
            grid=(t // tm,),
            in_specs=[
                pl.BlockSpec((tm, d), lambda i, s0, s1: (i, 0)),
                pl.BlockSpec((tm, 2), lambda i, s0, s1: (i, 0)),
                pl.BlockSpec((1, d), lambda i, s0, s1: (0, 0)),
                pl.BlockSpec(memory_space=pl.ANY),
            ],
            out_specs=pl.BlockSpec((tm, d), lambda i, s0, s1: (i, 0)),
            scratch_shapes=[pltpu.VMEM((2, tm, d // 2), jnp.uint32), pltpu.SemaphoreType.DMA],
        ),
        compiler_params=_cparams(("arbitrary",)),
        name="moe_combine",
    )(slot0, slot1, h, gates, final_gain.reshape(1, d), ys)


def _block_table(counts, t):
    counts = counts.astype(jnp.int32)
    padded = (counts + MOE_BLOCK - 1) // MOE_BLOCK * MOE_BLOCK
    pend = jnp.cumsum(padded)
    pstart = (pend - padded).astype(jnp.int32)
    n_blocks = (2 * t + MOE_BLOCK - 1) // MOE_BLOCK + N_EXPERTS
    block_start = jnp.arange(n_blocks, dtype=jnp.int32) * MOE_BLOCK
    block_expert = jnp.minimum(jnp.sum((pend[None, :] <= block_start[:, None]).astype(jnp.int32), axis=1),
                               N_EXPERTS - 1).astype(jnp.int32)
    n_used = (pend[-1] // MOE_BLOCK).astype(jnp.int32).reshape(1)
    filled_to = (pstart + counts)[block_expert]
    partial_block = ((block_start + MOE_BLOCK > filled_to) | (block_start >= pend[-1])).astype(jnp.int32)
    return pstart, block_expert, n_used, partial_block, n_blocks * MOE_BLOCK


def _moe(h, gain, w_group, b_group, w_expert, b_expert, w_gu, w_down, layer, final_gain, *,
         final, tm):
    t, _ = h.shape
    pad = ROUTER_WIDTH - N_GROUPS - N_EXPERTS
    w_router = jnp.concatenate([w_group, w_expert, jnp.zeros((w_group.shape[0], pad), F32)], axis=1)
    b_router = jnp.concatenate([b_group, b_expert, jnp.zeros((pad,), F32)], axis=0)
    xn, code, gates, counts = _router(h, gain, w_router, b_router, tm=tm)
    pstart, block_expert, n_used, partial_block, cap = _block_table(counts[0], t)
    expert = lax.shift_right_logical(code, RANK_BITS)
    onehot = expert[:, :, None] == jnp.arange(N_EXPERTS, dtype=jnp.int32)
    slot = jnp.sum(jnp.where(onehot, pstart, 0), axis=-1) + (code & ((1 << RANK_BITS) - 1))
    slot0, slot1 = slot[:, 0], slot[:, 1]
    xs = _dispatch(xn, partial_block, slot0, slot1, cap, tm=tm)
    ys = _experts(xs, w_gu, w_down, layer, block_expert, n_used)
    return _combine(h, gates, slot0, slot1, ys, final_gain, tm=tm, final=final)


def _pick(n, pref):
    for c in pref:
        if n % c == 0:
            return c
    return n


def kernel(x, norm_mix, norm_ffn, gdn_w_in, gdn_conv_w, gdn_a_log, gdn_dt_bias, gdn_norm_w,
           gdn_w_out, sc_w_in, sc_conv_w, sc_w_out, moe_w_group, moe_b_group, moe_w_expert,
           moe_b_expert, moe_w_gu, moe_w_down, norm_final):
    batch, seq, d = x.shape
    t = batch * seq
    h = x.reshape(t, d)
    tm = _pick(t, (1024, 512, 256))
    tb = _pick(seq, (256, 128, 64))

    w_in = gdn_w_in[0]
    proj, ba = _norm_matmul_side(h, norm_mix[0], w_in[:, :MAIN_DIM].astype(BF16),
                                 w_in[:, MAIN_DIM:], tm=tm, tn=1024, out_dtype=F32)
    o = _gdn_core(proj, ba, gdn_conv_w[0], gdn_a_log[0], gdn_dt_bias[0], gdn_norm_w[0],
                  batch=batch, seq=seq, tb=tb, hp=GDN_HEADS_PER_STEP)
    h = _matmul_res(o, gdn_w_out[0].astype(BF16), h, tm=tm, tn=1024)
    h = _moe(h, norm_ffn[0], moe_w_group[0], moe_b_group[0], moe_w_expert[0], moe_b_expert[0],
             moe_w_gu, moe_w_down, 0, norm_final, final=False, tm=_pick(t, (1024, 512, 256)))

    y = _sconv_proj(h, norm_mix[1], sc_w_in[0].astype(BF16), sc_conv_w[0], seq=seq,
                    tm=_pick(seq, (1024, 512, 256)), tn=512)
    h = _matmul_res(y, sc_w_out[0].astype(BF16), h, tm=tm, tn=1024)
    h = _moe(h, norm_ffn[1], moe_w_group[1], moe_b_group[1], moe_w_expert[1], moe_b_expert[1],
             moe_w_gu, moe_w_down, 1, norm_final, final=True, tm=_pick(t, (1024, 512, 256)))
    return h.reshape(batch, seq, d)
```

```python
import functools

import jax
import jax.numpy as jnp
from jax import lax
from jax.experimental import pallas as pl
from jax.experimental.pallas import tpu as pltpu

EPS = 1e-6
F32 = jnp.float32
BF16 = jnp.bfloat16

QK_HEADS = 16
V_HEADS = 32
HEAD_DIM = 128
KEY_DIM = QK_HEADS * HEAD_DIM
VAL_DIM = V_HEADS * HEAD_DIM
QKV_DIM = 2 * KEY_DIM + VAL_DIM
MAIN_DIM = QKV_DIM + VAL_DIM
GDN_CONV = 4
CHUNK = 64
HALO = 8
GDN_HEADS_PER_STEP = 4
GATE_ROWS = 16

SC_WIDTH = 3

N_GROUPS = 8
EXPERTS_PER_GROUP = 8
N_EXPERTS = N_GROUPS * EXPERTS_PER_GROUP
D_EXPERT = 512
MOE_BLOCK = 512
ROUTER_WIDTH = 128
RANK_BITS = 16

VMEM_LIMIT = 56 * 1024 * 1024


def _cparams(sem):
    return pltpu.CompilerParams(dimension_semantics=sem, vmem_limit_bytes=VMEM_LIMIT)


def _rms(x, gain):
    return x * lax.rsqrt(jnp.mean(x * x, axis=-1, keepdims=True) + EPS) * gain


def _sigmoid(x):
    return 0.5 * jnp.tanh(0.5 * x) + 0.5


def _softplus(x):
    return jnp.maximum(x, 0.0) + jnp.log1p(jnp.exp(-jnp.abs(x)))


def _pack_rows(v):
    half = v.shape[1] // 2
    bits = pltpu.bitcast(v, jnp.uint32)
    rounded = bits + jnp.uint32(0x7FFF) + ((bits >> 16) & jnp.uint32(1))
    return (rounded[:, :half] >> 16) | (rounded[:, half:] & jnp.uint32(0xFFFF0000))


def _unpack_rows(p):
    lo = pltpu.bitcast(p << 16, F32)
    hi = pltpu.bitcast(p & jnp.uint32(0xFFFF0000), F32)
    return lo, hi


def _split2(w):
    hi = w.astype(BF16)
    lo = (w - hi.astype(F32)).astype(BF16)
    return jnp.concatenate([hi, lo], axis=-1)


def _dot_split(x, w2):
    n = w2.shape[-1] // 2
    x_hi = x.astype(BF16)
    x_lo = (x - x_hi.astype(F32)).astype(BF16)
    a = jnp.dot(x_hi, w2, preferred_element_type=F32)
    b = jnp.dot(x_lo, w2[:, :n], preferred_element_type=F32)
    return (a[:, :n] + a[:, n:]) + b


def _norm_matmul_side_kernel(x_ref, g_ref, w_ref, ws_ref, o_ref, os_ref, xn_ref):
    @pl.when(pl.program_id(1) == 0)
    def _():
        xn = _rms(x_ref[...], g_ref[...])
        xn_ref[...] = xn.astype(BF16)
        os_ref[...] = _dot_split(xn, ws_ref[...])

    o_ref[...] = jnp.dot(xn_ref[...], w_ref[...], preferred_element_type=F32).astype(o_ref.dtype)


def _norm_matmul_side(x, gain, w, w_side, *, tm, tn, out_dtype):
    t, k = x.shape
    n, ns = w.shape[1], w_side.shape[1]
    return pl.pallas_call(
        _norm_matmul_side_kernel,
        out_shape=(jax.ShapeDtypeStruct((t, n), out_dtype), jax.ShapeDtypeStruct((t, ns), F32)),
        grid=(t // tm, n // tn),
        in_specs=[
            pl.BlockSpec((tm, k), lambda i, j: (i, 0)),
            pl.BlockSpec((1, k), lambda i, j: (0, 0)),
            pl.BlockSpec((k, tn), lambda i, j: (0, j)),
            pl.BlockSpec((k, 2 * ns), lambda i, j: (0, 0)),
        ],
        out_specs=(pl.BlockSpec((tm, tn), lambda i, j: (i, j)),
                   pl.BlockSpec((tm, ns), lambda i, j: (i, 0))),
        scratch_shapes=[pltpu.VMEM((tm, k), BF16)],
        compiler_params=_cparams(("parallel", "arbitrary")),
        name="norm_matmul_side",
    )(x, gain.reshape(1, k), w, _split2(w_side))


def _matmul_res_kernel(a_ref, w_ref, r_ref, o_ref):
    o_ref[...] = r_ref[...] + jnp.dot(a_ref[...], w_ref[...], preferred_element_type=F32)


def _matmul_res(a, w, res, *, tm, tn):
    t, k = a.shape
    n = w.shape[1]
    return pl.pallas_call(
        _matmul_res_kernel,
        out_shape=jax.ShapeDtypeStruct((t, n), F32),
        grid=(t // tm, n // tn),
        in_specs=[
            pl.BlockSpec((tm, k), lambda i, j: (i, 0)),
            pl.BlockSpec((k, tn), lambda i, j: (0, j)),
            pl.BlockSpec((tm, tn), lambda i, j: (i, j)),
        ],
        out_specs=pl.BlockSpec((tm, tn), lambda i, j: (i, j)),
        compiler_params=_cparams(("parallel", "arbitrary")),
        name="matmul_res",
    )(a, w, res)


def _gdn_kernel(q_ref, k_ref, v_ref, z_ref, row_ref, cwq_ref, cwk_ref, cwv_ref,
                alog_ref, dtb_ref, nw_ref, o_ref, state_ref, qbuf, kbuf, vbuf, *, tb, hp):
    nchunk = tb // CHUNK

    @pl.when(pl.program_id(2) == 0)
    def _():
        state_ref[...] = jnp.zeros_like(state_ref)
        qbuf[0:HALO, :] = jnp.zeros((HALO, qbuf.shape[1]), F32)
        kbuf[0:HALO, :] = jnp.zeros((HALO, kbuf.shape[1]), F32)
        vbuf[0:HALO, :] = jnp.zeros((HALO, vbuf.shape[1]), F32)

    def conv_silu(x_ref, buf, cw_ref):
        buf[HALO:HALO + tb, :] = x_ref[...].astype(F32)
        xb = buf[...]
        acc = cw_ref[0:1, :] * xb
        for kk in range(1, GDN_CONV):
            acc = cw_ref[kk:kk + 1, :] * xb + pltpu.roll(acc, 1, axis=0)
        buf[0:HALO, :] = buf[tb:tb + HALO, :]
        acc = acc[HALO:, :]
        return acc * _sigmoid(acc)

    def l2norm(x):
        return x * lax.rsqrt(jnp.sum(x * x, axis=-1, keepdims=True) + EPS)

    q_all = conv_silu(q_ref, qbuf, cwq_ref)
    k_all = conv_silu(k_ref, kbuf, cwk_ref)
    v_all = conv_silu(v_ref, vbuf, cwv_ref)

    ri = lax.broadcasted_iota(jnp.int32, (tb, tb), 0)
    ci = lax.broadcasted_iota(jnp.int32, (tb, tb), 1)
    same_chunk = (ri // CHUNK) == (ci // CHUNK)
    cum_tot = jnp.concatenate([jnp.where(same_chunk & (ri <= ci), 1.0, 0.0),
                               jnp.where(same_chunk, 1.0, 0.0)], axis=1).astype(BF16)
    si = lax.broadcasted_iota(jnp.int32, (3 * GATE_ROWS, GATE_ROWS), 0)
    sj = lax.broadcasted_iota(jnp.int32, (3 * GATE_ROWS, GATE_ROWS), 1)
    fold3 = jnp.where(si % GATE_ROWS == sj, 1.0, 0.0).astype(BF16)

    def split3(x):
        hi = x.astype(BF16).astype(F32)
        r1 = x - hi
        mid = r1.astype(BF16).astype(F32)
        lo = r1 - mid
        return jnp.concatenate([hi, mid, lo], axis=0).astype(BF16)

    r64 = lax.broadcasted_iota(jnp.int32, (CHUNK, CHUNK), 0)
    c64 = lax.broadcasted_iota(jnp.int32, (CHUNK, CHUNK), 1)
    tril = c64 <= r64
    strict = c64 < r64
    eye = jnp.where(c64 == r64, 1.0, 0.0).astype(F32)

    def mm(a, b):
        return lax.dot_general(a, b.astype(BF16), (((1,), (0,)), ((), ())),
                               preferred_element_type=F32)

    def mm_nt(a, b):
        return lax.dot_general(a.astype(BF16), b.astype(BF16), (((1,), (1,)), ((), ())),
                               preferred_element_type=F32)

    def mm_tn(a, b):
        return lax.dot_general(a.astype(BF16), b.astype(BF16), (((0,), (0,)), ((), ())),
                               preferred_element_type=F32)

    chunks = range(nchunk)
    rows = [slice(c * CHUNK, (c + 1) * CHUNK) for c in chunks]
    pairs = [(p, c) for p in range(hp) for c in chunks]
    heads = [(p, c, j) for p in range(hp) for c in chunks for j in range(2)]

    qc, kc, gate_col, gc_row = {}, {}, {}, {}
    for p in range(hp):
        hl = slice(p * HEAD_DIM, (p + 1) * HEAD_DIM)
        q = l2norm(q_all[:, hl]) * (HEAD_DIM ** -0.5)
        k = l2norm(k_all[:, hl])
        for c in chunks:
            qc[p, c], kc[p, c] = q[rows[c]], k[rows[c]]
        alog, dtb = alog_ref[p].reshape(2, 1), dtb_ref[p].reshape(2, 1)
        beta_r = _sigmoid(row_ref[p, 0:2, :])
        g_r = -jnp.exp(alog) * _softplus(row_ref[p, 2:4, :] + dtb)
        r3 = jnp.dot(split3(g_r), cum_tot, preferred_element_type=F32)
        r3 = (r3[0:2, :] + r3[2:4, :]) + r3[4:6, :]
        gc_r, gtot_r = r3[:, :tb], r3[:, tb:]
        egc_r = jnp.exp(gc_r)
        gc_row[p] = gc_r
        gate_rows = jnp.concatenate(
            [beta_r, gc_r, egc_r, jnp.exp(gtot_r - gc_r), jnp.exp(gtot_r), beta_r * egc_r,
             jnp.zeros((GATE_ROWS - 12, tb), F32)], axis=0)
        gate_col[p] = lax.dot_general(split3(gate_rows), fold3, (((0,), (0,)), ((), ())),
                                      preferred_element_type=F32)

    kk_t = {pc: mm_nt(kc[pc], kc[pc]) for pc in pairs}
    qk_t = {pc: mm_nt(qc[pc], kc[pc]) for pc in pairs}

    beta, egc, kdec, etot, bege, lower, attn = {}, {}, {}, {}, {}, {}, {}
    for h in heads:
        p, c, j = h
        col = lambda i: gate_col[p][rows[c], i + j:i + j + 1]
        beta[h], gcc, egc[h], kdec[h], bege[h] = col(0), col(2), col(4), col(6), col(10)
        etot[h] = col(8)[0:1, :]
        gcr = gc_row[p][j:j + 1, rows[c]]
        decay = jnp.exp(jnp.where(tril, gcc - gcr, -jnp.inf))
        lower[h] = jnp.where(strict, beta[h] * kk_t[p, c] * decay, 0.0)
        attn[h] = jnp.where(tril, qk_t[p, c] * decay, 0.0)

    inv = {h: eye - lower[h] for h in heads}
    power = dict(lower)
    for _ in range(5):
        power = {h: mm(power[h], power[h]) for h in heads}
        inv = {h: inv[h] + mm(inv[h], power[h]) for h in heads}

    uw = {}
    for h in heads:
        p, c, j = h
        vl = slice((2 * p + j) * HEAD_DIM, (2 * p + j + 1) * HEAD_DIM)
        rhs = jnp.concatenate([v_all[rows[c], vl] * beta[h], kc[p, c] * bege[h]], axis=1)
        uw[h] = mm(inv[h], rhs)
    nk = {h: mm_tn(kc[h[0], h[1]] * kdec[h], uw[h]) for h in heads}
    ao = {h: mm(attn[h], uw[h]) for h in heads}
    qp = {h: (qc[h[0], h[1]] * egc[h] - ao[h][:, HEAD_DIM:]).astype(BF16) for h in heads}

    state = {(p, j): state_ref[2 * p + j] for p in range(hp) for j in range(2)}
    seen = {}
    for c in chunks:
        for p in range(hp):
            for j in range(2):
                h = (p, c, j)
                s_bf = state[p, j].astype(BF16)
                seen[h] = s_bf
                n_c, k_c = nk[h][:, :HEAD_DIM], nk[h][:, HEAD_DIM:]
                state[p, j] = state[p, j] * etot[h] + (n_c - mm(k_c, s_bf))
    for p in range(hp):
        for j in range(2):
            state_ref[2 * p + j] = state[p, j]

    nw = nw_ref[...]
    for h in heads:
        p, c, j = h
        vl = slice((2 * p + j) * HEAD_DIM, (2 * p + j + 1) * HEAD_DIM)
        o = jnp.dot(qp[h], seen[h], preferred_element_type=F32) + ao[h][:, :HEAD_DIM]
        zc = z_ref[rows[c], vl].astype(F32)
        o = o * lax.rsqrt(jnp.mean(o * o, axis=-1, keepdims=True) + EPS) * nw
        o_ref[rows[c], vl] = (o * (zc * _sigmoid(zc))).astype(o_ref.dtype)


def _gdn_core(proj, ba, conv_w, a_log, dt_bias, norm_w, *, batch, seq, tb, hp):
    proj3 = proj.reshape(batch, seq, MAIN_DIM)
    ba4 = ba.reshape(batch, seq, 2, QK_HEADS, 2)
    rows = jnp.transpose(ba4, (0, 3, 2, 4, 1)).reshape(batch, QK_HEADS, 4, seq)
    alog2 = a_log.reshape(QK_HEADS, 1, 2)
    dtb2 = dt_bias.reshape(QK_HEADS, 1, 2)
    qw, vw = hp * HEAD_DIM, 2 * hp * HEAD_DIM
    kq = KEY_DIM // qw
    vq = (2 * KEY_DIM) // vw
    zq = QKV_DIM // vw
    out = pl.pallas_call(
        functools.partial(_gdn_kernel, tb=tb, hp=hp),
        out_shape=jax.ShapeDtypeStruct((batch, seq, VAL_DIM), BF16),
        grid=(batch, QK_HEADS // hp, seq // tb),
        in_specs=[
            pl.BlockSpec((None, tb, qw), lambda b, h, t: (b, t, h)),
            pl.BlockSpec((None, tb, qw), lambda b, h, t: (b, t, kq + h)),
            pl.BlockSpec((None, tb, vw), lambda b, h, t: (b, t, vq + h)),
            pl.BlockSpec((None, tb, vw), lambda b, h, t: (b, t, zq + h)),
            pl.BlockSpec((None, hp, 4, tb), lambda b, h, t: (b, h, 0, t)),
            pl.BlockSpec((GDN_CONV, qw), lambda b, h, t: (0, h)),
            pl.BlockSpec((GDN_CONV, qw), lambda b, h, t: (0, kq + h)),
            pl.BlockSpec((GDN_CONV, vw), lambda b, h, t: (0, vq + h)),
            pl.BlockSpec((hp, 1, 2), lambda b, h, t: (h, 0, 0)),
            pl.BlockSpec((hp, 1, 2), lambda b, h, t: (h, 0, 0)),
            pl.BlockSpec((1, HEAD_DIM), lambda b, h, t: (0, 0)),
        ],
        out_specs=pl.BlockSpec((None, tb, vw), lambda b, h, t: (b, t, h)),
        scratch_shapes=[
            pltpu.VMEM((2 * hp, HEAD_DIM, HEAD_DIM), F32),
            pltpu.VMEM((tb + HALO, qw), F32),
            pltpu.VMEM((tb + HALO, qw), F32),
            pltpu.VMEM((tb + HALO, vw), F32),
        ],
        compiler_params=_cparams(("parallel", "parallel", "arbitrary")),
        name="gdn_core",
    )(proj3, proj3, proj3, proj3, rows, conv_w, conv_w, conv_w, alog2, dtb2,
      norm_w.reshape(1, HEAD_DIM))
    return out.reshape(batch * seq, VAL_DIM)


def _sconv_proj_kernel(x_ref, g_ref, wb_ref, wc_ref, wh_ref, cw_ref, o_ref, xn_ref, halo_ref, buf,
                       *, tm, tiles_per_seq):
    i, j = pl.program_id(0), pl.program_id(1)

    @pl.when(j == 0)
    def _():
        xn_ref[...] = _rms(x_ref[...], g_ref[...]).astype(BF16)

    @pl.when(i % tiles_per_seq == 0)
    def _():
        halo_ref[j] = jnp.zeros((HALO, halo_ref.shape[2]), F32)

    xn = xn_ref[...]
    gate_b = jnp.dot(xn, wb_ref[...], preferred_element_type=F32)
    u = (jnp.dot(xn, wc_ref[...], preferred_element_type=F32)
         * jnp.dot(xn, wh_ref[...], preferred_element_type=F32))
    buf[0:HALO, :] = halo_ref[j]
    buf[HALO:HALO + tm, :] = u
    halo_ref[j] = u[tm - HALO:tm, :]
    ub = buf[...]
    acc = cw_ref[0:1, :] * ub
    for kk in range(1, SC_WIDTH):
        acc = cw_ref[kk:kk + 1, :] * ub + pltpu.roll(acc, 1, axis=0)
    o_ref[...] = (gate_b * acc[HALO:, :]).astype(o_ref.dtype)


def _sconv_proj(x, gain, w_in, conv_w, *, seq, tm, tn):
    t, k = x.shape
    d = conv_w.shape[1]
    nj = d // tn
    return pl.pallas_call(
        functools.partial(_sconv_proj_kernel, tm=tm, tiles_per_seq=seq // tm),
        out_shape=jax.ShapeDtypeStruct((t, d), BF16),
        grid=(t // tm, nj),
        in_specs=[
            pl.BlockSpec((tm, k), lambda i, j: (i, 0)),
            pl.BlockSpec((1, k), lambda i, j: (0, 0)),
            pl.BlockSpec((k, tn), lambda i, j: (0, j)),
            pl.BlockSpec((k, tn), lambda i, j: (0, nj + j)),
            pl.BlockSpec((k, tn), lambda i, j: (0, 2 * nj + j)),
            pl.BlockSpec((SC_WIDTH, tn), lambda i, j: (0, j)),
        ],
        out_specs=pl.BlockSpec((tm, tn), lambda i, j: (i, j)),
        scratch_shapes=[pltpu.VMEM((tm, k), BF16), pltpu.VMEM((nj, HALO, tn), F32),
                        pltpu.VMEM((tm + HALO, tn), F32)],
        compiler_params=_cparams(("arbitrary", "arbitrary")),
        name="sconv_proj",
    )(x, gain.reshape(1, k), w_in, w_in, w_in, conv_w)


def _router_kernel(h_ref, g_ref, wr_ref, br_ref, xn_ref, code_ref, gate_ref, cnt_ref, carry_ref):
    @pl.when(pl.program_id(0) == 0)
    def _():
        carry_ref[...] = jnp.zeros_like(carry_ref)

    xn = _rms(h_ref[...], g_ref[...])
    xn_ref[...] = _pack_rows(xn)
    logits = _dot_split(xn, wr_ref[...]) + br_ref[...]
    tm = logits.shape[0]
    glog = logits[:, 0:N_GROUPS]
    elog = logits[:, N_GROUPS:N_GROUPS + N_EXPERTS]
    gl = lax.broadcasted_iota(jnp.int32, (tm, N_GROUPS), 1)
    gmax = jnp.max(glog, axis=-1, keepdims=True)
    group = jnp.min(jnp.where(glog == gmax, gl, N_GROUPS), axis=-1, keepdims=True)
    p_group = 1.0 / jnp.sum(jnp.exp(glog - gmax), axis=-1, keepdims=True)
    el = lax.broadcasted_iota(jnp.int32, (tm, N_EXPERTS), 1)
    neg = jnp.float32(-jnp.inf)
    within = jnp.where((el // EXPERTS_PER_GROUP) == group, elog, neg)
    m1 = jnp.max(within, axis=-1, keepdims=True)
    i1 = jnp.min(jnp.where(within == m1, el, N_EXPERTS), axis=-1, keepdims=True)
    rest = jnp.where(el == i1, neg, within)
    m2 = jnp.max(rest, axis=-1, keepdims=True)
    i2 = jnp.min(jnp.where(rest == m2, el, N_EXPERTS), axis=-1, keepdims=True)
    e2 = jnp.exp(m2 - m1)
    g1 = p_group / (1.0 + e2)
    g2 = p_group * e2 / (1.0 + e2)

    pick1, pick2 = el == i1, el == i2
    onehot = jnp.where(pick1 | pick2, 1.0, 0.0)
    rr = lax.broadcasted_iota(jnp.int32, (tm, tm), 0)
    cc = lax.broadcasted_iota(jnp.int32, (tm, tm), 1)
    before = jnp.where(cc < rr, 1.0, 0.0).astype(BF16)
    prefix = jnp.dot(before, onehot.astype(BF16), preferred_element_type=F32) + carry_ref[...]
    r1 = jnp.sum(jnp.where(pick1, prefix, 0.0), axis=-1, keepdims=True).astype(jnp.int32)
    r2 = jnp.sum(jnp.where(pick2, prefix, 0.0), axis=-1, keepdims=True).astype(jnp.int32)
    carry_ref[...] = carry_ref[...] + jnp.sum(onehot, axis=0, keepdims=True)
    cnt_ref[...] = carry_ref[...]

    lane2 = lax.broadcasted_iota(jnp.int32, (tm, 2), 1)
    code1 = jnp.left_shift(i1, RANK_BITS) | r1
    code2 = jnp.left_shift(i2, RANK_BITS) | r2
    code_ref[...] = jnp.where(lane2 == 0, code1, code2)
    gate_ref[...] = jnp.where(lane2 == 0, g1, g2)


def _router(h, gain, w_router, b_router, *, tm):
    t, d = h.shape
    nr = w_router.shape[1]
    return pl.pallas_call(
        _router_kernel,
        out_shape=(jax.ShapeDtypeStruct((t, d // 2), jnp.uint32),
                   jax.ShapeDtypeStruct((t, 2), jnp.int32),
                   jax.ShapeDtypeStruct((t, 2), F32),
                   jax.ShapeDtypeStruct((1, N_EXPERTS), F32)),
        grid=(t // tm,),
        in_specs=[
            pl.BlockSpec((tm, d), lambda i: (i, 0)),
            pl.BlockSpec((1, d), lambda i: (0, 0)),
            pl.BlockSpec((d, 2 * nr), lambda i: (0, 0)),
            pl.BlockSpec((1, nr), lambda i: (0, 0)),
        ],
        out_specs=(pl.BlockSpec((tm, d // 2), lambda i: (i, 0)),
                   pl.BlockSpec((tm, 2), lambda i: (i, 0)),
                   pl.BlockSpec((tm, 2), lambda i: (i, 0)),
                   pl.BlockSpec((1, N_EXPERTS), lambda i: (0, 0))),
        scratch_shapes=[pltpu.VMEM((1, N_EXPERTS), F32)],
        compiler_params=_cparams(("arbitrary",)),
        name="moe_router",
    )(h, gain.reshape(1, d), _split2(w_router), b_router.reshape(1, nr))


def _dispatch_kernel(pf_ref, s0_ref, s1_ref, x_ref, xs_ref, zbuf, sem, zsem, *, tm, nb):
    base = pl.program_id(0) * tm

    @pl.when(pl.program_id(0) == 0)
    def _():
        zbuf[...] = jnp.zeros_like(zbuf)

        def zero_copy(b):
            return pltpu.make_async_copy(zbuf, xs_ref.at[pl.ds(b * MOE_BLOCK, MOE_BLOCK), :], zsem)

        def zstart(b, carry):
            @pl.when(pf_ref[b] != 0)
            def _():
                zero_copy(b).start()
            return carry

        def zwait(b, carry):
            @pl.when(pf_ref[b] != 0)
            def _():
                zero_copy(b).wait()
            return carry

        lax.fori_loop(0, nb, zstart, 0)
        lax.fori_loop(0, nb, zwait, 0)

    def row_copy(r, dest):
        return pltpu.make_async_copy(x_ref.at[pl.ds(r, 1), :], xs_ref.at[pl.ds(dest, 1), :], sem)

    def issue(r, carry):
        row_copy(r, s0_ref[base + r]).start(priority=0)
        row_copy(r, s1_ref[base + r]).start(priority=1)
        return carry

    lax.fori_loop(0, tm, issue, 0, unroll=8)
    whole = pltpu.make_async_copy(x_ref, xs_ref.at[pl.ds(0, tm), :], sem)
    whole.wait()
    whole.wait()


def _dispatch(xn, partial_block, slot0, slot1, cap, *, tm):
    t, d = xn.shape
    return pl.pallas_call(
        functools.partial(_dispatch_kernel, tm=tm, nb=cap // MOE_BLOCK),
        out_shape=jax.ShapeDtypeStruct((cap, d), xn.dtype),
        grid_spec=pltpu.PrefetchScalarGridSpec(
            num_scalar_prefetch=3,
            grid=(t // tm,),
            in_specs=[pl.BlockSpec((tm, d), lambda i, pf, s0, s1: (i, 0))],
            out_specs=pl.BlockSpec(memory_space=pl.ANY),
            scratch_shapes=[pltpu.VMEM((MOE_BLOCK, d), xn.dtype), pltpu.SemaphoreType.DMA,
                            pltpu.SemaphoreType.DMA],
        ),
        compiler_params=_cparams(("arbitrary",)),
        name="moe_dispatch",
    )(partial_block, slot0, slot1, xn)


def _expert_kernel(be_ref, nu_ref, x_ref, wgu_ref, wd_ref, o_ref, wgu_bf, wd_bf):
    i = pl.program_id(0)
    used = i < nu_ref[0]
    new_expert = (i == 0) | (be_ref[i] != be_ref[jnp.maximum(i - 1, 0)])

    @pl.when(used & new_expert)
    def _():
        wgu_bf[...] = wgu_ref[...].astype(BF16)
        wd_bf[...] = wd_ref[...].astype(BF16)

    @pl.when(used)
    def _():
        x_lo, x_hi = _unpack_rows(x_ref[...])
        half = x_lo.shape[1]
        gu = (jnp.dot(x_lo.astype(BF16), wgu_bf[0:half, :], preferred_element_type=F32)
              + jnp.dot(x_hi.astype(BF16), wgu_bf[half:, :], preferred_element_type=F32))
        gt, up = gu[:, :D_EXPERT], gu[:, D_EXPERT:]
        act = (gt * _sigmoid(gt) * up).astype(BF16)
        o_ref[...] = _pack_rows(jnp.dot(act, wd_bf[...], preferred_element_type=F32))

    @pl.when(jnp.logical_not(used))
    def _():
        o_ref[...] = jnp.zeros_like(o_ref)


def _experts(xs, w_gu, w_down, layer, block_expert, n_used):
    cap, dp = xs.shape
    d = 2 * dp
    nb = cap // MOE_BLOCK

    def blk(i, be, nu):
        return jnp.maximum(jnp.minimum(i, nu[0] - 1), 0)

    return pl.pallas_call(
        _expert_kernel,
        out_shape=jax.ShapeDtypeStruct((cap, dp), jnp.uint32),
        grid_spec=pltpu.PrefetchScalarGridSpec(
            num_scalar_prefetch=2,
            grid=(nb,),
            in_specs=[
                pl.BlockSpec((MOE_BLOCK, dp), lambda i, be, nu: (blk(i, be, nu), 0)),
                pl.BlockSpec((None, None, d, 2 * D_EXPERT),
                             lambda i, be, nu: (layer, be[blk(i, be, nu)], 0, 0)),
                pl.BlockSpec((None, None, D_EXPERT, d),
                             lambda i, be, nu: (layer, be[blk(i, be, nu)], 0, 0)),
            ],
            out_specs=pl.BlockSpec((MOE_BLOCK, dp), lambda i, be, nu: (i, 0)),
            scratch_shapes=[pltpu.VMEM((d, 2 * D_EXPERT), BF16), pltpu.VMEM((D_EXPERT, d), BF16)],
        ),
        compiler_params=_cparams(("arbitrary",)),
        name="moe_experts",
    )(block_expert, n_used, xs, w_gu, w_down)


def _combine_kernel(s0_ref, s1_ref, h_ref, gate_ref, fg_ref, ys_ref, o_ref, buf, sem, *,
                    tm, final):
    base = pl.program_id(0) * tm

    def row_copy(r, k, src):
        return pltpu.make_async_copy(ys_ref.at[pl.ds(src, 1), :], buf.at[k, pl.ds(r, 1), :], sem)

    def issue(r, carry):
        row_copy(r, 0, s0_ref[base + r]).start(priority=0)
        row_copy(r, 1, s1_ref[base + r]).start(priority=1)
        return carry

    lax.fori_loop(0, tm, issue, 0, unroll=8)
    for k in range(2):
        pltpu.make_async_copy(ys_ref.at[pl.ds(0, tm), :], buf.at[k], sem).wait()
    gate = gate_ref[...]
    lo0, hi0 = _unpack_rows(buf[0])
    lo1, hi1 = _unpack_rows(buf[1])
    g0, g1 = gate[:, 0:1], gate[:, 1:2]
    y = h_ref[...] + jnp.concatenate([g0 * lo0 + g1 * lo1, g0 * hi0 + g1 * hi1], axis=1)
    if final:
        y = _rms(y, fg_ref[...])
    o_ref[...] = y


def _combine(h, gates, slot0, slot1, ys, final_gain, *, tm, final):
    t, d = h.shape
    return pl.pallas_call(
        functools.partial(_combine_kernel, tm=tm, final=final),
        out_shape=jax.ShapeDtypeStruct((t, d), F32),
        grid_spec=pltpu.PrefetchScalarGridSpec(
            num_scalar_prefetch=2,
            grid=(t // tm,),
            in_specs=[
                pl.BlockSpec((tm, d), lambda i, s0, s1: (i, 0)),
                pl.BlockSpec((tm, 2), lambda i, s0, s1: (i, 0)),
                pl.BlockSpec((1, d), lambda i, s0, s1: (0, 0)),
                pl.BlockSpec(memory_space=pl.ANY),
            ],
            out_specs=pl.BlockSpec((tm, d), lambda i, s0, s1: (i, 0)),
            scratch_shapes=[pltpu.VMEM((2, tm, d // 2), jnp.uint32), pltpu.SemaphoreType.DMA],
        ),
        compiler_params=_cparams(("arbitrary",)),
        name="moe_combine",
    )(slot0, slot1, h, gates, final_gain.reshape(1, d), ys)


def _block_table(counts, t):
    counts = counts.astype(jnp.int32)
    padded = (counts + MOE_BLOCK - 1) // MOE_BLOCK * MOE_BLOCK
    pend = jnp.cumsum(padded)
    pstart = (pend - padded).astype(jnp.int32)
    n_blocks = (2 * t + MOE_BLOCK - 1) // MOE_BLOCK + N_EXPERTS
    block_start = jnp.arange(n_blocks, dtype=jnp.int32) * MOE_BLOCK
    block_expert = jnp.minimum(jnp.sum((pend[None, :] <= block_start[:, None]).astype(jnp.int32), axis=1),
                               N_EXPERTS - 1).astype(jnp.int32)
    n_used = (pend[-1] // MOE_BLOCK).astype(jnp.int32).reshape(1)
    filled_to = (pstart + counts)[block_expert]
    partial_block = ((block_start + MOE_BLOCK > filled_to) | (block_start >= pend[-1])).astype(jnp.int32)
    return pstart, block_expert, n_used, partial_block, n_blocks * MOE_BLOCK


def _moe(h, gain, w_group, b_group, w_expert, b_expert, w_gu, w_down, layer, final_gain, *,
         final, tm):
    t, _ = h.shape
    pad = ROUTER_WIDTH - N_GROUPS - N_EXPERTS
    w_router = jnp.concatenate([w_group, w_expert, jnp.zeros((w_group.shape[0], pad), F32)], axis=1)
    b_router = jnp.concatenate([b_group, b_expert, jnp.zeros((pad,), F32)], axis=0)
    xn, code, gates, counts = _router(h, gain, w_router, b_router, tm=tm)
    pstart, block_expert, n_used, partial_block, cap = _block_table(counts[0], t)
    expert = lax.shift_right_logical(code, RANK_BITS)
    onehot = expert[:, :, None] == jnp.arange(N_EXPERTS, dtype=jnp.int32)
    slot = jnp.sum(jnp.where(onehot, pstart, 0), axis=-1) + (code & ((1 << RANK_BITS) - 1))
    slot0, slot1 = slot[:, 0], slot[:, 1]
    xs = _dispatch(xn, partial_block, slot0, slot1, cap, tm=tm)
    ys = _experts(xs, w_gu, w_down, layer, block_expert, n_used)
    return _combine(h, gates, slot0, slot1, ys, final_gain, tm=tm, final=final)


def _pick(n, pref):
    for c in pref:
        if n % c == 0:
            return c
    return n


def kernel(x, norm_mix, norm_ffn, gdn_w_in, gdn_conv_w, gdn_a_log, gdn_dt_bias, gdn_norm_w,
           gdn_w_out, sc_w_in, sc_conv_w, sc_w_out, moe_w_group, moe_b_group, moe_w_expert,
           moe_b_expert, moe_w_gu, moe_w_down, norm_final):
    batch, seq, d = x.shape
    t = batch * seq
    h = x.reshape(t, d)
    tm = _pick(t, (1024, 512, 256))
    tb = _pick(seq, (256, 128, 64))

    w_in = gdn_w_in[0]
    proj, ba = _norm_matmul_side(h, norm_mix[0], w_in[:, :MAIN_DIM].astype(BF16),
                                 w_in[:, MAIN_DIM:], tm=tm, tn=1024, out_dtype=F32)
    o = _gdn_core(proj, ba, gdn_conv_w[0], gdn_a_log[0], gdn_dt_bias[0], gdn_norm_w[0],
                  batch=batch, seq=seq, tb=tb, hp=GDN_HEADS_PER_STEP)
    h = _matmul_res(o, gdn_w_out[0].astype(BF16), h, tm=tm, tn=1024)
    h = _moe(h, norm_ffn[0], moe_w_group[0], moe_b_group[0], moe_w_expert[0], moe_b_expert[0],
             moe_w_gu, moe_w_down, 0, norm_final, final=False, tm=_pick(t, (512, 256)))

    y = _sconv_proj(h, norm_mix[1], sc_w_in[0].astype(BF16), sc_conv_w[0], seq=seq,
                    tm=_pick(seq, (1024, 512, 256)), tn=512)
    h = _matmul_res(y, sc_w_out[0].astype(BF16), h, tm=tm, tn=1024)
    h = _moe(h, norm_ffn[1], moe_w_group[1], moe_b_group[1], moe_w_expert[1], moe_b_expert[1],
             moe_w_gu, moe_w_down, 1, norm_final, final=True, tm=_pick(t, (512, 256)))
    return h.reshape(batch, seq, d)
```

```python
import functools

import jax
import jax.numpy as jnp
from jax import lax
from jax.experimental import pallas as pl
from jax.experimental.pallas import tpu as pltpu

EPS = 1e-6
F32 = jnp.float32
BF16 = jnp.bfloat16

QK_HEADS = 16
V_HEADS = 32
HEAD_DIM = 128
KEY_DIM = QK_HEADS * HEAD_DIM
VAL_DIM = V_HEADS * HEAD_DIM
QKV_DIM = 2 * KEY_DIM + VAL_DIM
MAIN_DIM = QKV_DIM + VAL_DIM
GDN_CONV = 4
CHUNK = 64
HALO = 8
GDN_HEADS_PER_STEP = 4
GATE_ROWS = 16

SC_WIDTH = 3

N_GROUPS = 8
EXPERTS_PER_GROUP = 8
N_EXPERTS = N_GROUPS * EXPERTS_PER_GROUP
D_EXPERT = 512
MOE_BLOCK = 512
ROUTER_WIDTH = 128
RANK_BITS = 16

VMEM_LIMIT = 56 * 1024 * 1024


def _cparams(sem):
    return pltpu.CompilerParams(dimension_semantics=sem, vmem_limit_bytes=VMEM_LIMIT)


def _rms(x, gain):
    return x * lax.rsqrt(jnp.mean(x * x, axis=-1, keepdims=True) + EPS) * gain


def _sigmoid(x):
    return 0.5 * jnp.tanh(0.5 * x) + 0.5


def _silu(x):
    h = 0.5 * x
    return h + h * jnp.tanh(h)


def _softplus(x):
    return jnp.maximum(x, 0.0) + jnp.log1p(jnp.exp(-jnp.abs(x)))


def _pack_rows(v):
    half = v.shape[1] // 2
    bits = pltpu.bitcast(v, jnp.uint32)
    rounded = bits + jnp.uint32(0x7FFF) + ((bits >> 16) & jnp.uint32(1))
    return (rounded[:, :half] >> 16) | (rounded[:, half:] & jnp.uint32(0xFFFF0000))


def _unpack_rows(p):
    lo = pltpu.bitcast(p << 16, F32)
    hi = pltpu.bitcast(p & jnp.uint32(0xFFFF0000), F32)
    return lo, hi


def _split2(w):
    hi = w.astype(BF16)
    lo = (w - hi.astype(F32)).astype(BF16)
    return jnp.concatenate([hi, lo], axis=-1)


def _dot_split(x, w2):
    n = w2.shape[-1] // 2
    x_hi = x.astype(BF16)
    x_lo = (x - x_hi.astype(F32)).astype(BF16)
    a = jnp.dot(x_hi, w2, preferred_element_type=F32)
    b = jnp.dot(x_lo, w2[:, :n], preferred_element_type=F32)
    return (a[:, :n] + a[:, n:]) + b


def _norm_matmul_side_kernel(x_ref, g_ref, w_ref, ws_ref, o_ref, os_ref, xn_ref):
    @pl.when(pl.program_id(1) == 0)
    def _():
        xn = _rms(x_ref[...], g_ref[...])
        xn_ref[...] = xn.astype(BF16)
        os_ref[...] = _dot_split(xn, ws_ref[...])

    o_ref[...] = jnp.dot(xn_ref[...], w_ref[...], preferred_element_type=F32).astype(o_ref.dtype)


def _norm_matmul_side(x, gain, w, w_side, *, tm, tn, out_dtype):
    t, k = x.shape
    n, ns = w.shape[1], w_side.shape[1]
    return pl.pallas_call(
        _norm_matmul_side_kernel,
        out_shape=(jax.ShapeDtypeStruct((t, n), out_dtype), jax.ShapeDtypeStruct((t, ns), F32)),
        grid=(t // tm, n // tn),
        in_specs=[
            pl.BlockSpec((tm, k), lambda i, j: (i, 0)),
            pl.BlockSpec((1, k), lambda i, j: (0, 0)),
            pl.BlockSpec((k, tn), lambda i, j: (0, j)),
            pl.BlockSpec((k, 2 * ns), lambda i, j: (0, 0)),
        ],
        out_specs=(pl.BlockSpec((tm, tn), lambda i, j: (i, j)),
                   pl.BlockSpec((tm, ns), lambda i, j: (i, 0))),
        scratch_shapes=[pltpu.VMEM((tm, k), BF16)],
        compiler_params=_cparams(("parallel", "arbitrary")),
        name="norm_matmul_side",
    )(x, gain.reshape(1, k), w, _split2(w_side))


def _matmul_res_kernel(a_ref, w_ref, r_ref, o_ref):
    o_ref[...] = r_ref[...] + jnp.dot(a_ref[...], w_ref[...], preferred_element_type=F32)


def _matmul_res(a, w, res, *, tm, tn):
    t, k = a.shape
    n = w.shape[1]
    return pl.pallas_call(
        _matmul_res_kernel,
        out_shape=jax.ShapeDtypeStruct((t, n), F32),
        grid=(t // tm, n // tn),
        in_specs=[
            pl.BlockSpec((tm, k), lambda i, j: (i, 0)),
            pl.BlockSpec((k, tn), lambda i, j: (0, j)),
            pl.BlockSpec((tm, tn), lambda i, j: (i, j)),
        ],
        out_specs=pl.BlockSpec((tm, tn), lambda i, j: (i, j)),
        compiler_params=_cparams(("parallel", "arbitrary")),
        name="matmul_res",
    )(a, w, res)


def _gdn_kernel(q_ref, k_ref, v_ref, z_ref, row_ref, cwq_ref, cwk_ref, cwv_ref,
                alog_ref, dtb_ref, nw_ref, o_ref, state_ref, qbuf, kbuf, vbuf, *, tb, hp):
    nchunk = tb // CHUNK

    @pl.when(pl.program_id(2) == 0)
    def _():
        state_ref[...] = jnp.zeros_like(state_ref)
        qbuf[0:HALO, :] = jnp.zeros((HALO, qbuf.shape[1]), F32)
        kbuf[0:HALO, :] = jnp.zeros((HALO, kbuf.shape[1]), F32)
        vbuf[0:HALO, :] = jnp.zeros((HALO, vbuf.shape[1]), F32)

    def conv_silu(x_ref, buf, cw_ref):
        buf[HALO:HALO + tb, :] = x_ref[...].astype(F32)
        xb = buf[...]
        acc = cw_ref[0:1, :] * xb
        for kk in range(1, GDN_CONV):
            acc = cw_ref[kk:kk + 1, :] * xb + pltpu.roll(acc, 1, axis=0)
        buf[0:HALO, :] = buf[tb:tb + HALO, :]
        acc = acc[HALO:, :]
        return _silu(acc)

    def l2norm(x):
        return x * lax.rsqrt(jnp.sum(x * x, axis=-1, keepdims=True) + EPS)

    q_all = conv_silu(q_ref, qbuf, cwq_ref)
    k_all = conv_silu(k_ref, kbuf, cwk_ref)
    v_all = conv_silu(v_ref, vbuf, cwv_ref)

    ri = lax.broadcasted_iota(jnp.int32, (tb, tb), 0)
    ci = lax.broadcasted_iota(jnp.int32, (tb, tb), 1)
    same_chunk = (ri // CHUNK) == (ci // CHUNK)
    cum_tot = jnp.concatenate([jnp.where(same_chunk & (ri <= ci), 1.0, 0.0),
                               jnp.where(same_chunk, 1.0, 0.0)], axis=1).astype(BF16)
    si = lax.broadcasted_iota(jnp.int32, (3 * GATE_ROWS, GATE_ROWS), 0)
    sj = lax.broadcasted_iota(jnp.int32, (3 * GATE_ROWS, GATE_ROWS), 1)
    fold3 = jnp.where(si % GATE_ROWS == sj, 1.0, 0.0).astype(BF16)

    def split3(x):
        hi = x.astype(BF16).astype(F32)
        r1 = x - hi
        mid = r1.astype(BF16).astype(F32)
        lo = r1 - mid
        return jnp.concatenate([hi, mid, lo], axis=0).astype(BF16)

    r64 = lax.broadcasted_iota(jnp.int32, (CHUNK, CHUNK), 0)
    c64 = lax.broadcasted_iota(jnp.int32, (CHUNK, CHUNK), 1)
    tril = c64 <= r64
    strict = c64 < r64
    eye = jnp.where(c64 == r64, 1.0, 0.0).astype(F32)

    def mm(a, b):
        return lax.dot_general(a, b.astype(BF16), (((1,), (0,)), ((), ())),
                               preferred_element_type=F32)

    def mm_nt(a, b):
        return lax.dot_general(a.astype(BF16), b.astype(BF16), (((1,), (1,)), ((), ())),
                               preferred_element_type=F32)

    def mm_tn(a, b):
        return lax.dot_general(a.astype(BF16), b.astype(BF16), (((0,), (0,)), ((), ())),
                               preferred_element_type=F32)

    chunks = range(nchunk)
    rows = [slice(c * CHUNK, (c + 1) * CHUNK) for c in chunks]
    pairs = [(p, c) for p in range(hp) for c in chunks]
    heads = [(p, c, j) for p in range(hp) for c in chunks for j in range(2)]

    qc, kc, gate_col, gc_row = {}, {}, {}, {}
    for p in range(hp):
        hl = slice(p * HEAD_DIM, (p + 1) * HEAD_DIM)
        q = l2norm(q_all[:, hl]) * (HEAD_DIM ** -0.5)
        k = l2norm(k_all[:, hl])
        for c in chunks:
            qc[p, c], kc[p, c] = q[rows[c]], k[rows[c]]
        alog, dtb = alog_ref[p].reshape(2, 1), dtb_ref[p].reshape(2, 1)
        beta_r = _sigmoid(row_ref[p, 0:2, :])
        g_r = -jnp.exp(alog) * _softplus(row_ref[p, 2:4, :] + dtb)
        r3 = jnp.dot(split3(g_r), cum_tot, preferred_element_type=F32)
        r3 = (r3[0:2, :] + r3[2:4, :]) + r3[4:6, :]
        gc_r, gtot_r = r3[:, :tb], r3[:, tb:]
        egc_r = jnp.exp(gc_r)
        gc_row[p] = gc_r
        gate_rows = jnp.concatenate(
            [beta_r, gc_r, egc_r, jnp.exp(gtot_r - gc_r), jnp.exp(gtot_r), beta_r * egc_r,
             jnp.zeros((GATE_ROWS - 12, tb), F32)], axis=0)
        gate_col[p] = lax.dot_general(split3(gate_rows), fold3, (((0,), (0,)), ((), ())),
                                      preferred_element_type=F32)

    kk_t = {pc: mm_nt(kc[pc], kc[pc]) for pc in pairs}
    qk_t = {pc: mm_nt(qc[pc], kc[pc]) for pc in pairs}

    beta, egc, kdec, etot, bege, lower, attn = {}, {}, {}, {}, {}, {}, {}
    for h in heads:
        p, c, j = h
        col = lambda i: gate_col[p][rows[c], i + j:i + j + 1]
        beta[h], gcc, egc[h], kdec[h], bege[h] = col(0), col(2), col(4), col(6), col(10)
        etot[h] = col(8)[0:1, :]
        gcr = gc_row[p][j:j + 1, rows[c]]
        decay = jnp.exp(jnp.where(tril, gcc - gcr, -jnp.inf))
        lower[h] = jnp.where(strict, beta[h] * kk_t[p, c] * decay, 0.0)
        attn[h] = jnp.where(tril, qk_t[p, c] * decay, 0.0)

    inv = {h: eye - lower[h] for h in heads}
    power = dict(lower)
    for _ in range(5):
        power = {h: mm(power[h], power[h]) for h in heads}
        inv = {h: inv[h] + mm(inv[h], power[h]) for h in heads}

    uw = {}
    for h in heads:
        p, c, j = h
        vl = slice((2 * p + j) * HEAD_DIM, (2 * p + j + 1) * HEAD_DIM)
        rhs = jnp.concatenate([v_all[rows[c], vl] * beta[h], kc[p, c] * bege[h]], axis=1)
        uw[h] = mm(inv[h], rhs)
    nk = {h: mm_tn(kc[h[0], h[1]] * kdec[h], uw[h]) for h in heads}
    ao = {h: mm(attn[h], uw[h]) for h in heads}
    qp = {h: (qc[h[0], h[1]] * egc[h] - ao[h][:, HEAD_DIM:]).astype(BF16) for h in heads}

    state = {(p, j): state_ref[2 * p + j] for p in range(hp) for j in range(2)}
    seen = {}
    for c in chunks:
        for p in range(hp):
            for j in range(2):
                h = (p, c, j)
                s_bf = state[p, j].astype(BF16)
                seen[h] = s_bf
                n_c, k_c = nk[h][:, :HEAD_DIM], nk[h][:, HEAD_DIM:]
                state[p, j] = state[p, j] * etot[h] + (n_c - mm(k_c, s_bf))
    for p in range(hp):
        for j in range(2):
            state_ref[2 * p + j] = state[p, j]

    nw = nw_ref[...]
    for h in heads:
        p, c, j = h
        vl = slice((2 * p + j) * HEAD_DIM, (2 * p + j + 1) * HEAD_DIM)
        o = jnp.dot(qp[h], seen[h], preferred_element_type=F32) + ao[h][:, :HEAD_DIM]
        zc = z_ref[rows[c], vl].astype(F32)
        o = o * lax.rsqrt(jnp.mean(o * o, axis=-1, keepdims=True) + EPS) * nw
        o_ref[rows[c], vl] = (o * _silu(zc)).astype(o_ref.dtype)


def _gdn_core(proj, ba, conv_w, a_log, dt_bias, norm_w, *, batch, seq, tb, hp):
    proj3 = proj.reshape(batch, seq, MAIN_DIM)
    ba4 = ba.reshape(batch, seq, 2, QK_HEADS, 2)
    rows = jnp.transpose(ba4, (0, 3, 2, 4, 1)).reshape(batch, QK_HEADS, 4, seq)
    alog2 = a_log.reshape(QK_HEADS, 1, 2)
    dtb2 = dt_bias.reshape(QK_HEADS, 1, 2)
    qw, vw = hp * HEAD_DIM, 2 * hp * HEAD_DIM
    kq = KEY_DIM // qw
    vq = (2 * KEY_DIM) // vw
    zq = QKV_DIM // vw
    out = pl.pallas_call(
        functools.partial(_gdn_kernel, tb=tb, hp=hp),
        out_shape=jax.ShapeDtypeStruct((batch, seq, VAL_DIM), BF16),
        grid=(batch, QK_HEADS // hp, seq // tb),
        in_specs=[
            pl.BlockSpec((None, tb, qw), lambda b, h, t: (b, t, h)),
            pl.BlockSpec((None, tb, qw), lambda b, h, t: (b, t, kq + h)),
            pl.BlockSpec((None, tb, vw), lambda b, h, t: (b, t, vq + h)),
            pl.BlockSpec((None, tb, vw), lambda b, h, t: (b, t, zq + h)),
            pl.BlockSpec((None, hp, 4, tb), lambda b, h, t: (b, h, 0, t)),
            pl.BlockSpec((GDN_CONV, qw), lambda b, h, t: (0, h)),
            pl.BlockSpec((GDN_CONV, qw), lambda b, h, t: (0, kq + h)),
            pl.BlockSpec((GDN_CONV, vw), lambda b, h, t: (0, vq + h)),
            pl.BlockSpec((hp, 1, 2), lambda b, h, t: (h, 0, 0)),
            pl.BlockSpec((hp, 1, 2), lambda b, h, t: (h, 0, 0)),
            pl.BlockSpec((1, HEAD_DIM), lambda b, h, t: (0, 0)),
        ],
        out_specs=pl.BlockSpec((None, tb, vw), lambda b, h, t: (b, t, h)),
        scratch_shapes=[
            pltpu.VMEM((2 * hp, HEAD_DIM, HEAD_DIM), F32),
            pltpu.VMEM((tb + HALO, qw), F32),
            pltpu.VMEM((tb + HALO, qw), F32),
            pltpu.VMEM((tb + HALO, vw), F32),
        ],
        compiler_params=_cparams(("parallel", "parallel", "arbitrary")),
        name="gdn_core",
    )(proj3, proj3, proj3, proj3, rows, conv_w, conv_w, conv_w, alog2, dtb2,
      norm_w.reshape(1, HEAD_DIM))
    return out.reshape(batch * seq, VAL_DIM)


def _sconv_proj_kernel(x_ref, g_ref, wb_ref, wc_ref, wh_ref, cw_ref, o_ref, xn_ref, halo_ref, buf,
                       *, tm, tiles_per_seq):
    i, j = pl.program_id(0), pl.program_id(1)

    @pl.when(j == 0)
    def _():
        xn_ref[...] = _rms(x_ref[...], g_ref[...]).astype(BF16)

    @pl.when(i % tiles_per_seq == 0)
    def _():
        halo_ref[j] = jnp.zeros((HALO, halo_ref.shape[2]), F32)

    xn = xn_ref[...]
    gate_b = jnp.dot(xn, wb_ref[...], preferred_element_type=F32)
    u = (jnp.dot(xn, wc_ref[...], preferred_element_type=F32)
         * jnp.dot(xn, wh_ref[...], preferred_element_type=F32))
    buf[0:HALO, :] = halo_ref[j]
    buf[HALO:HALO + tm, :] = u
    halo_ref[j] = u[tm - HALO:tm, :]
    ub = buf[...]
    acc = cw_ref[0:1, :] * ub
    for kk in range(1, SC_WIDTH):
        acc = cw_ref[kk:kk + 1, :] * ub + pltpu.roll(acc, 1, axis=0)
    o_ref[...] = (gate_b * acc[HALO:, :]).astype(o_ref.dtype)


def _sconv_proj(x, gain, w_in, conv_w, *, seq, tm, tn):
    t, k = x.shape
    d = conv_w.shape[1]
    nj = d // tn
    return pl.pallas_call(
        functools.partial(_sconv_proj_kernel, tm=tm, tiles_per_seq=seq // tm),
        out_shape=jax.ShapeDtypeStruct((t, d), BF16),
        grid=(t // tm, nj),
        in_specs=[
            pl.BlockSpec((tm, k), lambda i, j: (i, 0)),
            pl.BlockSpec((1, k), lambda i, j: (0, 0)),
            pl.BlockSpec((k, tn), lambda i, j: (0, j)),
            pl.BlockSpec((k, tn), lambda i, j: (0, nj + j)),
            pl.BlockSpec((k, tn), lambda i, j: (0, 2 * nj + j)),
            pl.BlockSpec((SC_WIDTH, tn), lambda i, j: (0, j)),
        ],
        out_specs=pl.BlockSpec((tm, tn), lambda i, j: (i, j)),
        scratch_shapes=[pltpu.VMEM((tm, k), BF16), pltpu.VMEM((nj, HALO, tn), F32),
                        pltpu.VMEM((tm + HALO, tn), F32)],
        compiler_params=_cparams(("arbitrary", "arbitrary")),
        name="sconv_proj",
    )(x, gain.reshape(1, k), w_in, w_in, w_in, conv_w)


def _router_kernel(h_ref, g_ref, wr_ref, br_ref, xn_ref, code_ref, gate_ref, cnt_ref, carry_ref):
    @pl.when(pl.program_id(0) == 0)
    def _():
        carry_ref[...] = jnp.zeros_like(carry_ref)

    xn = _rms(h_ref[...], g_ref[...])
    xn_ref[...] = _pack_rows(xn)
    logits = _dot_split(xn, wr_ref[...]) + br_ref[...]
    tm = logits.shape[0]
    glog = logits[:, 0:N_GROUPS]
    elog = logits[:, N_GROUPS:N_GROUPS + N_EXPERTS]
    gl = lax.broadcasted_iota(jnp.int32, (tm, N_GROUPS), 1)
    gmax = jnp.max(glog, axis=-1, keepdims=True)
    group = jnp.min(jnp.where(glog == gmax, gl, N_GROUPS), axis=-1, keepdims=True)
    p_group = 1.0 / jnp.sum(jnp.exp(glog - gmax), axis=-1, keepdims=True)
    el = lax.broadcasted_iota(jnp.int32, (tm, N_EXPERTS), 1)
    neg = jnp.float32(-jnp.inf)
    within = jnp.where((el // EXPERTS_PER_GROUP) == group, elog, neg)
    m1 = jnp.max(within, axis=-1, keepdims=True)
    i1 = jnp.min(jnp.where(within == m1, el, N_EXPERTS), axis=-1, keepdims=True)
    rest = jnp.where(el == i1, neg, within)
    m2 = jnp.max(rest, axis=-1, keepdims=True)
    i2 = jnp.min(jnp.where(rest == m2, el, N_EXPERTS), axis=-1, keepdims=True)
    e2 = jnp.exp(m2 - m1)
    g1 = p_group / (1.0 + e2)
    g2 = p_group * e2 / (1.0 + e2)

    pick1, pick2 = el == i1, el == i2
    onehot = jnp.where(pick1 | pick2, 1.0, 0.0)
    rr = lax.broadcasted_iota(jnp.int32, (tm, tm), 0)
    cc = lax.broadcasted_iota(jnp.int32, (tm, tm), 1)
    before = jnp.where(cc < rr, 1.0, 0.0).astype(BF16)
    prefix = jnp.dot(before, onehot.astype(BF16), preferred_element_type=F32) + carry_ref[...]
    r1 = jnp.sum(jnp.where(pick1, prefix, 0.0), axis=-1, keepdims=True).astype(jnp.int32)
    r2 = jnp.sum(jnp.where(pick2, prefix, 0.0), axis=-1, keepdims=True).astype(jnp.int32)
    carry_ref[...] = carry_ref[...] + jnp.sum(onehot, axis=0, keepdims=True)
    cnt_ref[...] = carry_ref[...]

    lane2 = lax.broadcasted_iota(jnp.int32, (tm, 2), 1)
    code1 = jnp.left_shift(i1, RANK_BITS) | r1
    code2 = jnp.left_shift(i2, RANK_BITS) | r2
    code_ref[...] = jnp.where(lane2 == 0, code1, code2)
    gate_ref[...] = jnp.where(lane2 == 0, g1, g2)


def _router(h, gain, w_router, b_router, *, tm):
    t, d = h.shape
    nr = w_router.shape[1]
    return pl.pallas_call(
        _router_kernel,
        out_shape=(jax.ShapeDtypeStruct((t, d // 2), jnp.uint32),
                   jax.ShapeDtypeStruct((t, 2), jnp.int32),
                   jax.ShapeDtypeStruct((t, 2), F32),
                   jax.ShapeDtypeStruct((1, N_EXPERTS), F32)),
        grid=(t // tm,),
        in_specs=[
            pl.BlockSpec((tm, d), lambda i: (i, 0)),
            pl.BlockSpec((1, d), lambda i: (0, 0)),
            pl.BlockSpec((d, 2 * nr), lambda i: (0, 0)),
            pl.BlockSpec((1, nr), lambda i: (0, 0)),
        ],
        out_specs=(pl.BlockSpec((tm, d // 2), lambda i: (i, 0)),
                   pl.BlockSpec((tm, 2), lambda i: (i, 0)),
                   pl.BlockSpec((tm, 2), lambda i: (i, 0)),
                   pl.BlockSpec((1, N_EXPERTS), lambda i: (0, 0))),
        scratch_shapes=[pltpu.VMEM((1, N_EXPERTS), F32)],
        compiler_params=_cparams(("arbitrary",)),
        name="moe_router",
    )(h, gain.reshape(1, d), _split2(w_router), b_router.reshape(1, nr))


def _dispatch_kernel(pf_ref, s0_ref, s1_ref, x_ref, xs_ref, zbuf, sem, zsem, *, tm, nb):
    base = pl.program_id(0) * tm

    @pl.when(pl.program_id(0) == 0)
    def _():
        zbuf[...] = jnp.zeros_like(zbuf)

        def zero_copy(b):
            return pltpu.make_async_copy(zbuf, xs_ref.at[pl.ds(b * MOE_BLOCK, MOE_BLOCK), :], zsem)

        def zstart(b, carry):
            @pl.when(pf_ref[b] != 0)
            def _():
                zero_copy(b).start()
            return carry

        def zwait(b, carry):
            @pl.when(pf_ref[b] != 0)
            def _():
                zero_copy(b).wait()
            return carry

        lax.fori_loop(0, nb, zstart, 0)
        lax.fori_loop(0, nb, zwait, 0)

    def row_copy(r, dest):
        return pltpu.make_async_copy(x_ref.at[pl.ds(r, 1), :], xs_ref.at[pl.ds(dest, 1), :], sem)

    def issue(r, carry):
        row_copy(r, s0_ref[base + r]).start(priority=0)
        row_copy(r, s1_ref[base + r]).start(priority=1)
        return carry

    lax.fori_loop(0, tm, issue, 0, unroll=8)
    whole = pltpu.make_async_copy(x_ref, xs_ref.at[pl.ds(0, tm), :], sem)
    whole.wait()
    whole.wait()


def _dispatch(xn, partial_block, slot0, slot1, cap, *, tm):
    t, d = xn.shape
    return pl.pallas_call(
        functools.partial(_dispatch_kernel, tm=tm, nb=cap // MOE_BLOCK),
        out_shape=jax.ShapeDtypeStruct((cap, d), xn.dtype),
        grid_spec=pltpu.PrefetchScalarGridSpec(
            num_scalar_prefetch=3,
            grid=(t // tm,),
            in_specs=[pl.BlockSpec((tm, d), lambda i, pf, s0, s1: (i, 0))],
            out_specs=pl.BlockSpec(memory_space=pl.ANY),
            scratch_shapes=[pltpu.VMEM((MOE_BLOCK, d), xn.dtype), pltpu.SemaphoreType.DMA,
                            pltpu.SemaphoreType.DMA],
        ),
        compiler_params=_cparams(("arbitrary",)),
        name="moe_dispatch",
    )(partial_block, slot0, slot1, xn)


def _expert_kernel(be_ref, nu_ref, x_ref, wgu_ref, wd_ref, o_ref, wgu_bf, wd_bf):
    i = pl.program_id(0)
    used = i < nu_ref[0]
    new_expert = (i == 0) | (be_ref[i] != be_ref[jnp.maximum(i - 1, 0)])

    @pl.when(used & new_expert)
    def _():
        wgu_bf[...] = wgu_ref[...].astype(BF16)
        wd_bf[...] = wd_ref[...].astype(BF16)

    @pl.when(used)
    def _():
        x_lo, x_hi = _unpack_rows(x_ref[...])
        half = x_lo.shape[1]
        gu = (jnp.dot(x_lo.astype(BF16), wgu_bf[0:half, :], preferred_element_type=F32)
              + jnp.dot(x_hi.astype(BF16), wgu_bf[half:, :], preferred_element_type=F32))
        gt, up = gu[:, :D_EXPERT], gu[:, D_EXPERT:]
        act = (_silu(gt) * up).astype(BF16)
        o_ref[...] = _pack_rows(jnp.dot(act, wd_bf[...], preferred_element_type=F32))

    @pl.when(jnp.logical_not(used))
    def _():
        o_ref[...] = jnp.zeros_like(o_ref)


def _experts(xs, w_gu, w_down, layer, block_expert, n_used):
    cap, dp = xs.shape
    d = 2 * dp
    nb = cap // MOE_BLOCK

    def blk(i, be, nu):
        return jnp.maximum(jnp.minimum(i, nu[0] - 1), 0)

    return pl.pallas_call(
        _expert_kernel,
        out_shape=jax.ShapeDtypeStruct((cap, dp), jnp.uint32),
        grid_spec=pltpu.PrefetchScalarGridSpec(
            num_scalar_prefetch=2,
            grid=(nb,),
            in_specs=[
                pl.BlockSpec((MOE_BLOCK, dp), lambda i, be, nu: (blk(i, be, nu), 0)),
                pl.BlockSpec((None, None, d, 2 * D_EXPERT),
                             lambda i, be, nu: (layer, be[blk(i, be, nu)], 0, 0)),
                pl.BlockSpec((None, None, D_EXPERT, d),
                             lambda i, be, nu: (layer, be[blk(i, be, nu)], 0, 0)),
            ],
            out_specs=pl.BlockSpec((MOE_BLOCK, dp), lambda i, be, nu: (i, 0)),
            scratch_shapes=[pltpu.VMEM((d, 2 * D_EXPERT), BF16), pltpu.VMEM((D_EXPERT, d), BF16)],
        ),
        compiler_params=_cparams(("arbitrary",)),
        name="moe_experts",
    )(block_expert, n_used, xs, w_gu, w_down)


def _combine_kernel(s0_ref, s1_ref, h_ref, gate_ref, fg_ref, ys_ref, o_ref, buf, sem, *,
                    tm, final):
    base = pl.program_id(0) * tm

    def row_copy(r, k, src):
        return pltpu.make_async_copy(ys_ref.at[pl.ds(src, 1), :], buf.at[k, pl.ds(r, 1), :], sem)

    def issue(r, carry):
        row_copy(r, 0, s0_ref[base + r]).start(priority=0)
        row_copy(r, 1, s1_ref[base + r]).start(priority=1)
        return carry

    lax.fori_loop(0, tm, issue, 0, unroll=8)
    for k in range(2):
        pltpu.make_async_copy(ys_ref.at[pl.ds(0, tm), :], buf.at[k], sem).wait()
    gate = gate_ref[...]
    lo0, hi0 = _unpack_rows(buf[0])
    lo1, hi1 = _unpack_rows(buf[1])
    g0, g1 = gate[:, 0:1], gate[:, 1:2]
    y = h_ref[...] + jnp.concatenate([g0 * lo0 + g1 * lo1, g0 * hi0 + g1 * hi1], axis=1)
    if final:
        y = _rms(y, fg_ref[...])
    o_ref[...] = y


def _combine(h, gates, slot0, slot1, ys, final_gain, *, tm, final):
    t, d = h.shape
    return pl.pallas_call(
        functools.partial(_combine_kernel, tm=tm, final=final),
        out_shape=jax.ShapeDtypeStruct((t, d), F32),
        grid_spec=pltpu.PrefetchScalarGridSpec(
            num_scalar_prefetch=2,
            grid=(t // tm,),
            in_specs=[
                pl.BlockSpec((tm, d), lambda i, s0, s1: (i, 0)),
                pl.BlockSpec((tm, 2), lambda i, s0, s1: (i, 0)),
                pl.BlockSpec((1, d), lambda i, s0, s1: (0, 0)),
                pl.BlockSpec(memory_space=pl.ANY),
            ],
            out_specs=pl.BlockSpec((tm, d), lambda i, s0, s1: (i, 0)),
            scratch_shapes=[pltpu.VMEM((2, tm, d // 2), jnp.uint32), pltpu.SemaphoreType.DMA],
        ),
        compiler_params=_cparams(("arbitrary",)),
        name="moe_combine",
    )(slot0, slot1, h, gates, final_gain.reshape(1, d), ys)


def _block_table(counts, t):
    counts = counts.astype(jnp.int32)
    padded = (counts + MOE_BLOCK - 1) // MOE_BLOCK * MOE_BLOCK
    pend = jnp.cumsum(padded)
    pstart = (pend - padded).astype(jnp.int32)
    n_blocks = (2 * t + MOE_BLOCK - 1) // MOE_BLOCK + N_EXPERTS
    block_start = jnp.arange(n_blocks, dtype=jnp.int32) * MOE_BLOCK
    block_expert = jnp.minimum(jnp.sum((pend[None, :] <= block_start[:, None]).astype(jnp.int32), axis=1),
                               N_EXPERTS - 1).astype(jnp.int32)
    n_used = (pend[-1] // MOE_BLOCK).astype(jnp.int32).reshape(1)
    filled_to = (pstart + counts)[block_expert]
    partial_block = ((block_start + MOE_BLOCK > filled_to) | (block_start >= pend[-1])).astype(jnp.int32)
    return pstart, block_expert, n_used, partial_block, n_blocks * MOE_BLOCK


def _moe(h, gain, w_group, b_group, w_expert, b_expert, w_gu, w_down, layer, final_gain, *,
         final, tm):
    t, _ = h.shape
    pad = ROUTER_WIDTH - N_GROUPS - N_EXPERTS
    w_router = jnp.concatenate([w_group, w_expert, jnp.zeros((w_group.shape[0], pad), F32)], axis=1)
    b_router = jnp.concatenate([b_group, b_expert, jnp.zeros((pad,), F32)], axis=0)
    xn, code, gates, counts = _router(h, gain, w_router, b_router, tm=tm)
    pstart, block_expert, n_used, partial_block, cap = _block_table(counts[0], t)
    expert = lax.shift_right_logical(code, RANK_BITS)
    onehot = expert[:, :, None] == jnp.arange(N_EXPERTS, dtype=jnp.int32)
    slot = jnp.sum(jnp.where(onehot, pstart, 0), axis=-1) + (code & ((1 << RANK_BITS) - 1))
    slot0, slot1 = slot[:, 0], slot[:, 1]
    xs = _dispatch(xn, partial_block, slot0, slot1, cap, tm=tm)
    ys = _experts(xs, w_gu, w_down, layer, block_expert, n_used)
    return _combine(h, gates, slot0, slot1, ys, final_gain, tm=tm, final=final)


def _pick(n, pref):
    for c in pref:
        if n % c == 0:
            return c
    return n


def kernel(x, norm_mix, norm_ffn, gdn_w_in, gdn_conv_w, gdn_a_log, gdn_dt_bias, gdn_norm_w,
           gdn_w_out, sc_w_in, sc_conv_w, sc_w_out, moe_w_group, moe_b_group, moe_w_expert,
           moe_b_expert, moe_w_gu, moe_w_down, norm_final):
    batch, seq, d = x.shape
    t = batch * seq
    h = x.reshape(t, d)
    tm = _pick(t, (1024, 512, 256))
    tb = _pick(seq, (256, 128, 64))

    w_in = gdn_w_in[0]
    proj, ba = _norm_matmul_side(h, norm_mix[0], w_in[:, :MAIN_DIM].astype(BF16),
                                 w_in[:, MAIN_DIM:], tm=tm, tn=1024, out_dtype=F32)
    o = _gdn_core(proj, ba, gdn_conv_w[0], gdn_a_log[0], gdn_dt_bias[0], gdn_norm_w[0],
                  batch=batch, seq=seq, tb=tb, hp=GDN_HEADS_PER_STEP)
    h = _matmul_res(o, gdn_w_out[0].astype(BF16), h, tm=tm, tn=1024)
    h = _moe(h, norm_ffn[0], moe_w_group[0], moe_b_group[0], moe_w_expert[0], moe_b_expert[0],
             moe_w_gu, moe_w_down, 0, norm_final, final=False, tm=_pick(t, (512, 256)))

    y = _sconv_proj(h, norm_mix[1], sc_w_in[0].astype(BF16), sc_conv_w[0], seq=seq,
                    tm=_pick(seq, (1024, 512, 256)), tn=512)
    h = _matmul_res(y, sc_w_out[0].astype(BF16), h, tm=tm, tn=1024)
    h = _moe(h, norm_ffn[1], moe_w_group[1], moe_b_group[1], moe_w_expert[1], moe_b_expert[1],
             moe_w_gu, moe_w_down, 1, norm_final, final=True, tm=_pick(t, (512, 256)))
    return h.reshape(batch, seq, d)
```
